```python
import functools
import jax, jax.numpy as jnp
from jax import lax
import numpy as np

D_MODEL = 2048
BATCH = 16
SEQ = 2048
DEPTH = 1
DEC_BATCH = 128
DEC_SEQ = 4
PAST_LEN = 16384
PAGE_SIZE = 128

HEAD_DIM = 64
N_Q_HEADS = 16
N_KV_HEADS = 4
Q_PER_KV = N_Q_HEADS // N_KV_HEADS
ATTN_WIDTH = N_Q_HEADS * HEAD_DIM
KV_WIDTH = N_KV_HEADS * HEAD_DIM
WINDOW = 128
ROPE_THETA = 10000.0
CONV_CH = D_MODEL // 2
CONV_WIDTH = 31
MIX_WIDTH = ATTN_WIDTH + CONV_CH
IN_WIDTH = ATTN_WIDTH + 2 * KV_WIDTH + 2 * CONV_CH
N_EXPERTS = 256
TOP_K = 8
N_EXPERT_GROUPS = 8
TOPK_GROUPS = 4
EXPERTS_PER_GROUP = N_EXPERTS // N_EXPERT_GROUPS
EXPERT_FF = 512
SHARED_FF = 512
ROUTED_SCALE = 2.5
DISPATCH_BLOCK = 128
LN_EPS = 1e-5
RMS_EPS = 1e-6
DEEPNORM_ALPHA = (2.0 * DEPTH) ** 0.25
DEEPNORM_BETA = (8.0 * DEPTH) ** -0.25

kernel_name = "hymba_swa_sink_conformer_conv_moe_deepnorm_step"


def layer_norm(x, g, b):
    xf = x.astype(jnp.float32)
    xc = xf - xf.mean(-1, keepdims=True)
    var = (xc * xc).mean(-1, keepdims=True)
    return (xc * lax.rsqrt(var + LN_EPS) * g.astype(jnp.float32) + b.astype(jnp.float32)).astype(x.dtype)


def rms_norm(x, g):
    xf = x.astype(jnp.float32)
    ms = (xf * xf).mean(-1, keepdims=True)
    return (xf * lax.rsqrt(ms + RMS_EPS) * g.astype(jnp.float32)).astype(x.dtype)


def rope(x, pos):
    half = HEAD_DIM // 2
    inv_freq = 1.0 / (ROPE_THETA ** (jnp.arange(half, dtype=jnp.float32) * (2.0 / HEAD_DIM)))
    ang = pos.astype(jnp.float32)[:, None] * inv_freq[None, :]
    cos = jnp.cos(ang)[:, None, :]
    sin = jnp.sin(ang)[:, None, :]
    xf = x.astype(jnp.float32)
    x1, x2 = xf[..., :half], xf[..., half:]
    return jnp.concatenate([x1 * cos - x2 * sin, x2 * cos + x1 * sin], axis=-1).astype(x.dtype)


def window_mask(qpos, kpos):
    qp = qpos[..., :, None]
    kp = kpos[..., None, :]
    return (kp <= qp) & (qp - kp < WINDOW) & (kp >= 0)


def sink_attention(q, k, v, mask, sinks):
    s = jnp.einsum('...qkgd,...skd->...kgqs', q, k).astype(jnp.float32) * (HEAD_DIM ** -0.5)
    s = jnp.where(mask, s, -jnp.inf)
    sink = sinks.astype(jnp.float32).reshape(N_KV_HEADS, Q_PER_KV, 1, 1)
    m = jnp.maximum(s.max(-1, keepdims=True), sink)
    e = jnp.exp(s - m)
    p = e / (e.sum(-1, keepdims=True) + jnp.exp(sink - m))
    return jnp.einsum('...kgqs,...skd->...qkgd', p.astype(v.dtype), v)


def prompt_attend(q, k, v, sinks):
    b, s = q.shape[0], q.shape[1]
    nb = s // WINDOW
    qb = q.reshape(b, nb, WINDOW, N_KV_HEADS, Q_PER_KV, HEAD_DIM)
    kb = k.reshape(b, nb, WINDOW, N_KV_HEADS, HEAD_DIM)
    vb = v.reshape(b, nb, WINDOW, N_KV_HEADS, HEAD_DIM)
    prev = lambda t: jnp.concatenate([jnp.zeros_like(t[:, :1]), t[:, :-1]], axis=1)
    k_band = jnp.concatenate([prev(kb), kb], axis=2)
    v_band = jnp.concatenate([prev(vb), vb], axis=2)
    blk = jnp.arange(nb)[:, None]
    qpos = blk * WINDOW + jnp.arange(WINDOW)[None, :]
    kpos = (blk - 1) * WINDOW + jnp.arange(2 * WINDOW)[None, :]
    mask = window_mask(qpos, kpos)[:, None, None]
    o = sink_attention(qb, k_band, v_band, mask, sinks).reshape(b, s, ATTN_WIDTH)
    keep = min(WINDOW, s)
    return o, k[:, -keep:], v[:, -keep:]


def sample_attend(q, k, v, sinks, k_cache, v_cache):
    b, t = q.shape[0], q.shape[1]
    lc = k_cache.shape[1]
    k_all = jnp.concatenate([k_cache, k], axis=1)
    v_all = jnp.concatenate([v_cache, v], axis=1)
    qpos = PAST_LEN + jnp.arange(t)
    kpos = PAST_LEN - lc + jnp.arange(lc + t)
    mask = window_mask(qpos, kpos)
    qg = q.reshape(b, t, N_KV_HEADS, Q_PER_KV, HEAD_DIM)
    o = sink_attention(qg, k_all, v_all, mask, sinks).reshape(b, t, ATTN_WIDTH)
    keep = min(WINDOW, PAST_LEN + t)
    return o, k_all[:, -keep:], v_all[:, -keep:]


def conv_branch(u, hist, conv_w, conv_b, ln_g, ln_b):
    a, gate = jnp.split(u, 2, axis=-1)
    g = a * jax.nn.sigmoid(gate)
    ext = jnp.concatenate([hist, g], axis=1)
    y = lax.conv_general_dilated(ext, conv_w[:, None, :], window_strides=(1,), padding='VALID',
                                 dimension_numbers=('NWC', 'WIO', 'NWC'),
                                 feature_group_count=CONV_CH) + conv_b
    y = jax.nn.silu(layer_norm(y, ln_g, ln_b))
    return y, ext[:, -(CONV_WIDTH - 1):]


def moe_ffn(x, router_w, router_b, w_gate, w_up, w_down, sh_gate, sh_up, sh_down):
    t = x.shape[0]
    scores = jax.nn.sigmoid((x @ router_w).astype(jnp.float32))
    sel = scores + router_b.astype(jnp.float32)
    grp_score = lax.top_k(sel.reshape(t, N_EXPERT_GROUPS, EXPERTS_PER_GROUP), 2)[0].sum(-1)
    _, gidx = lax.top_k(grp_score, TOPK_GROUPS)
    gmask = (gidx[:, :, None] == jnp.arange(N_EXPERT_GROUPS)[None, None, :]).any(1)
    emask = jnp.repeat(gmask, EXPERTS_PER_GROUP, axis=1)
    _, eidx = lax.top_k(jnp.where(emask, sel, -jnp.inf), TOP_K)
    gates = jnp.take_along_axis(scores, eidx, axis=1)
    gates = gates / gates.sum(-1, keepdims=True) * ROUTED_SCALE
    tk = t * TOP_K
    n_blocks = -(-tk // DISPATCH_BLOCK) + N_EXPERTS
    n_rows = n_blocks * DISPATCH_BLOCK
    flat_e = eidx.reshape(-1)
    flat_tok = jnp.repeat(jnp.arange(t, dtype=jnp.int32), TOP_K)
    flat_g = gates.reshape(-1)
    order = jnp.argsort(flat_e, stable=True)
    sorted_e = flat_e[order]
    counts = jnp.bincount(flat_e, length=N_EXPERTS)
    start = jnp.cumsum(counts) - counts
    pcounts = (counts + DISPATCH_BLOCK - 1) // DISPATCH_BLOCK * DISPATCH_BLOCK
    pend = jnp.cumsum(pcounts)
    pstart = pend - pcounts
    dest = pstart[sorted_e] + (jnp.arange(tk) - start[sorted_e])
    row_tok = jnp.full((n_rows,), t, jnp.int32).at[dest].set(flat_tok[order])
    row_gate = jnp.zeros((n_rows,), jnp.float32).at[dest].set(flat_g[order])
    block_e = jnp.minimum(jnp.searchsorted(pend, jnp.arange(n_blocks) * DISPATCH_BLOCK, side='right'),
                          N_EXPERTS - 1)
    x_pad = jnp.concatenate([x, jnp.zeros((1, x.shape[1]), x.dtype)], axis=0)

    def expert_block(args):
        tok, e, g = args
        xb = x_pad[tok]
        hb = jax.nn.silu(xb @ w_gate[e]) * (xb @ w_up[e])
        return (hb @ w_down[e]) * g[:, None].astype(xb.dtype)

    yb = lax.map(expert_block, (row_tok.reshape(n_blocks, DISPATCH_BLOCK), block_e,
                                row_gate.reshape(n_blocks, DISPATCH_BLOCK)))
    routed = jnp.zeros((t + 1, x.shape[1]), x.dtype).at[row_tok].add(yb.reshape(n_rows, x.shape[1]))[:t]
    shared = (jax.nn.silu(x @ sh_gate) * (x @ sh_up)) @ sh_down
    return routed + shared


def decoder_layer(x, pos, attend, conv_hist, w):
    b, t, _ = x.shape
    proj = x @ w['w_in']
    q, k, v, u = jnp.split(proj, [ATTN_WIDTH, ATTN_WIDTH + KV_WIDTH, ATTN_WIDTH + 2 * KV_WIDTH], axis=-1)
    q = rope(q.reshape(b, t, N_Q_HEADS, HEAD_DIM), pos)
    k = rope(k.reshape(b, t, N_KV_HEADS, HEAD_DIM), pos)
    v = v.reshape(b, t, N_KV_HEADS, HEAD_DIM)
    attn, k_state, v_state = attend(q, k, v, w['attn_sinks'])
    conv, conv_state = conv_branch(u, conv_hist, w['conv_w'], w['conv_b'], w['conv_ln_g'], w['conv_ln_b'])
    mixed = jnp.concatenate([rms_norm(attn, w['attn_out_g']), rms_norm(conv, w['conv_out_g'])], axis=-1)
    h = layer_norm(DEEPNORM_ALPHA * x + mixed @ w['w_out'], w['ln1_g'], w['ln1_b'])
    f = moe_ffn(h.reshape(b * t, D_MODEL), w['router_w'], w['router_b'], w['exp_w_gate'], w['exp_w_up'],
                w['exp_w_down'], w['sh_w_gate'], w['sh_w_up'], w['sh_w_down']).reshape(b, t, D_MODEL)
    y = layer_norm(DEEPNORM_ALPHA * h + f, w['ln2_g'], w['ln2_b'])
    return y, k_state, v_state, conv_state


def setup_inputs(seed: int = 0) -> dict:
    key = jax.random.key(seed)
    ks = jax.random.split(key, 32)
    f32 = jnp.float32
    nrm = lambda k, shape, scale: jax.random.normal(k, shape, f32) * scale
    win_len = min(WINDOW, PAST_LEN)
    return {
        'x_prompt': nrm(ks[0], (BATCH, SEQ, D_MODEL), 1.0),
        'x_sample': nrm(ks[1], (DEC_BATCH, DEC_SEQ, D_MODEL), 1.0),
        'cache_win_k': nrm(ks[2], (DEPTH, DEC_BATCH, win_len, N_KV_HEADS, HEAD_DIM), 1.0),
        'cache_win_v': nrm(ks[3], (DEPTH, DEC_BATCH, win_len, N_KV_HEADS, HEAD_DIM), 1.0),
        'state_conv': nrm(ks[4], (DEPTH, DEC_BATCH, CONV_WIDTH - 1, CONV_CH), 0.5),
        'w_in': nrm(ks[5], (DEPTH, D_MODEL, IN_WIDTH), D_MODEL ** -0.5),
        'attn_sinks': nrm(ks[6], (DEPTH, N_Q_HEADS), 1.0),
        'conv_w': nrm(ks[7], (DEPTH, CONV_WIDTH, CONV_CH), CONV_WIDTH ** -0.5),
        'conv_b': nrm(ks[8], (DEPTH, CONV_CH), 0.02),
        'conv_ln_g': 1.0 + nrm(ks[9], (DEPTH, CONV_CH), 0.02),
        'conv_ln_b': nrm(ks[10], (DEPTH, CONV_CH), 0.02),
        'attn_out_g': 1.0 + nrm(ks[11], (DEPTH, ATTN_WIDTH), 0.02),
        'conv_out_g': 1.0 + nrm(ks[12], (DEPTH, CONV_CH), 0.02),
        'w_out': nrm(ks[13], (DEPTH, MIX_WIDTH, D_MODEL), MIX_WIDTH ** -0.5 * DEEPNORM_BETA),
        'ln1_g': 1.0 + nrm(ks[14], (DEPTH, D_MODEL), 0.02),
        'ln1_b': nrm(ks[15], (DEPTH, D_MODEL), 0.02),
        'router_w': nrm(ks[16], (DEPTH, D_MODEL, N_EXPERTS), D_MODEL ** -0.5),
        'router_b': nrm(ks[17], (DEPTH, N_EXPERTS), 0.01),
        'exp_w_gate': nrm(ks[18], (DEPTH, N_EXPERTS, D_MODEL, EXPERT_FF), D_MODEL ** -0.5),
        'exp_w_up': nrm(ks[19], (DEPTH, N_EXPERTS, D_MODEL, EXPERT_FF), D_MODEL ** -0.5),
        'exp_w_down': nrm(ks[20], (DEPTH, N_EXPERTS, EXPERT_FF, D_MODEL), EXPERT_FF ** -0.5 * DEEPNORM_BETA),
        'sh_w_gate': nrm(ks[21], (DEPTH, D_MODEL, SHARED_FF), D_MODEL ** -0.5),
        'sh_w_up': nrm(ks[22], (DEPTH, D_MODEL, SHARED_FF), D_MODEL ** -0.5),
        'sh_w_down': nrm(ks[23], (DEPTH, SHARED_FF, D_MODEL), SHARED_FF ** -0.5 * DEEPNORM_BETA),
        'ln2_g': 1.0 + nrm(ks[24], (DEPTH, D_MODEL), 0.02),
        'ln2_b': nrm(ks[25], (DEPTH, D_MODEL), 0.02),
    }


def reference(x_prompt, x_sample, cache_win_k, cache_win_v, state_conv, w_in, attn_sinks, conv_w, conv_b,
              conv_ln_g, conv_ln_b, attn_out_g, conv_out_g, w_out, ln1_g, ln1_b, router_w, router_b,
              exp_w_gate, exp_w_up, exp_w_down, sh_w_gate, sh_w_up, sh_w_down, ln2_g, ln2_b):
    xp, xs = x_prompt, x_sample
    pos_prompt = jnp.arange(xp.shape[1])
    pos_sample = PAST_LEN + jnp.arange(xs.shape[1])
    kp_l, vp_l, cp_l, ks_l, vs_l, cs_l = [], [], [], [], [], []
    for l in range(DEPTH):
        w = dict(w_in=w_in[l], attn_sinks=attn_sinks[l], conv_w=conv_w[l], conv_b=conv_b[l],
                 conv_ln_g=conv_ln_g[l], conv_ln_b=conv_ln_b[l], attn_out_g=attn_out_g[l],
                 conv_out_g=conv_out_g[l], w_out=w_out[l], ln1_g=ln1_g[l], ln1_b=ln1_b[l],
                 router_w=router_w[l], router_b=router_b[l], exp_w_gate=exp_w_gate[l],
                 exp_w_up=exp_w_up[l], exp_w_down=exp_w_down[l], sh_w_gate=sh_w_gate[l],
                 sh_w_up=sh_w_up[l], sh_w_down=sh_w_down[l], ln2_g=ln2_g[l], ln2_b=ln2_b[l])
        zero_hist = jnp.zeros((xp.shape[0], CONV_WIDTH - 1, CONV_CH), xp.dtype)
        xp, kp, vp, cp = decoder_layer(xp, pos_prompt, prompt_attend, zero_hist, w)
        attend_s = functools.partial(sample_attend, k_cache=cache_win_k[l], v_cache=cache_win_v[l])
        xs, kss, vss, css = decoder_layer(xs, pos_sample, attend_s, state_conv[l], w)
        kp_l.append(kp); vp_l.append(vp); cp_l.append(cp)
        ks_l.append(kss); vs_l.append(vss); cs_l.append(css)
    return (xp, xs, jnp.stack(kp_l), jnp.stack(vp_l), jnp.stack(cp_l),
            jnp.stack(ks_l), jnp.stack(vs_l), jnp.stack(cs_l))
```

```python
import functools

import jax
import jax.numpy as jnp
from jax import lax
from jax.experimental import pallas as pl
from jax.experimental.pallas import tpu as pltpu

F32 = jnp.float32
BF16 = jnp.bfloat16

DEPTH = 1
PAST_LEN = 16384
HEAD_DIM = 64
N_Q_HEADS = 16
N_KV_HEADS = 4
Q_PER_KV = N_Q_HEADS // N_KV_HEADS
ATTN_WIDTH = N_Q_HEADS * HEAD_DIM
KV_WIDTH = N_KV_HEADS * HEAD_DIM
WINDOW = 128
ROPE_THETA = 10000.0
CONV_WIDTH = 31
CONV_HALO = 32
TOP_K = 8
N_EXPERT_GROUPS = 8
TOPK_GROUPS = 4
ROUTED_SCALE = 2.5
LN_EPS = 1e-5
RMS_EPS = 1e-6
DEEPNORM_ALPHA = (2.0 * DEPTH) ** 0.25

LANES = 128
VMEM_LIMIT = 56 * 1024 * 1024
ROW_TILE = 512
EXPERT_TILE = 256
SORT_SHIFT = 19


def _cparams(*sem):
    return pltpu.CompilerParams(dimension_semantics=sem, vmem_limit_bytes=VMEM_LIMIT)


def _dot(a, b):
    return jnp.dot(a, b, preferred_element_type=F32)


def _layer_norm(x, g, b):
    xc = x - jnp.mean(x, axis=-1, keepdims=True)
    var = jnp.mean(xc * xc, axis=-1, keepdims=True)
    return xc * lax.rsqrt(var + LN_EPS) * g + b


def _rms_norm(x, g):
    ms = jnp.mean(x * x, axis=-1, keepdims=True)
    return x * lax.rsqrt(ms + RMS_EPS) * g


def _sigmoid(x):
    return 1.0 / (1.0 + jnp.exp(-x))


def _rope_tables(pos):
    half = HEAD_DIM // 2
    inv_freq = 1.0 / (ROPE_THETA ** (jnp.arange(half, dtype=F32) * (2.0 / HEAD_DIM)))
    ang = pos.astype(F32)[:, None] * inv_freq[None, :]
    cos, sin = jnp.cos(ang), jnp.sin(ang)
    reps = LANES // HEAD_DIM
    return jnp.tile(cos, (1, 2 * reps)), jnp.tile(jnp.concatenate([-sin, sin], axis=1), (1, reps))


def _in_proj_kernel(x_ref, w_ref, cos_ref, sin_ref, q_ref, k_ref, v_ref, g_ref):
    xb = x_ref[...].astype(BF16)
    cos = cos_ref[...]
    sin = sin_ref[...]
    lane = lax.broadcasted_iota(jnp.int32, cos.shape, 1)
    first_half = (lane % HEAD_DIM) < (HEAD_DIM // 2)

    def rope(acc):
        outs = []
        for c in range(acc.shape[1] // LANES):
            a = acc[:, c * LANES:(c + 1) * LANES]
            partner = jnp.where(first_half, pltpu.roll(a, LANES - HEAD_DIM // 2, 1),
                                pltpu.roll(a, HEAD_DIM // 2, 1))
            outs.append(a * cos + partner * sin)
        return jnp.concatenate(outs, axis=1)

    o_k = ATTN_WIDTH
    o_v = o_k + KV_WIDTH
    o_a = o_v + KV_WIDTH
    ch = (w_ref.shape[1] - o_a) // 2
    q = rope(_dot(xb, w_ref[:, 0:o_k]))
    q_ref[...] = (q * (HEAD_DIM ** -0.5)).astype(BF16)
    k_ref[...] = rope(_dot(xb, w_ref[:, o_k:o_v]))
    v_ref[...] = _dot(xb, w_ref[:, o_v:o_a])
    a = _dot(xb, w_ref[:, o_a:o_a + ch])
    gate = _dot(xb, w_ref[:, o_a + ch:o_a + 2 * ch])
    g_ref[...] = a * _sigmoid(gate)


def _in_proj(x2d, w_bf16, cos, sin, tm):
    t, d = x2d.shape
    n = w_bf16.shape[1]
    ch = (n - ATTN_WIDTH - 2 * KV_WIDTH) // 2
    ntab = cos.shape[0] // tm
    return pl.pallas_call(
        _in_proj_kernel,
        grid=(t // tm,),
        in_specs=[
            pl.BlockSpec((tm, d), lambda i: (i, 0)),
            pl.BlockSpec((d, n), lambda i: (0, 0), pipeline_mode=pl.Buffered(1)),
            pl.BlockSpec((tm, LANES), lambda i: (i % ntab, 0)),
            pl.BlockSpec((tm, LANES), lambda i: (i % ntab, 0)),
        ],
        out_specs=[
            pl.BlockSpec((tm, ATTN_WIDTH), lambda i: (i, 0)),
            pl.BlockSpec((tm, KV_WIDTH), lambda i: (i, 0)),
            pl.BlockSpec((tm, KV_WIDTH), lambda i: (i, 0)),
            pl.BlockSpec((tm, ch), lambda i: (i, 0)),
        ],
        out_shape=[
            jax.ShapeDtypeStruct((t, ATTN_WIDTH), BF16),
            jax.ShapeDtypeStruct((t, KV_WIDTH), F32),
            jax.ShapeDtypeStruct((t, KV_WIDTH), F32),
            jax.ShapeDtypeStruct((t, ch), F32),
        ],
        compiler_params=_cparams("parallel"),
        name="in_proj",
    )(x2d, w_bf16, cos, sin)


def _sink_softmax_pv(s, mask, sink, v_bf16):
    s = jnp.where(mask, s, -jnp.inf)
    m = jnp.maximum(jnp.max(s, axis=-1, keepdims=True), sink)
    e = jnp.exp(s - m)
    den = jnp.sum(e, axis=-1, keepdims=True) + jnp.exp(sink - m)
    return _dot(e.astype(BF16), v_bf16) / den


def _prompt_attn_kernel(sink_ref, q_ref, kc_ref, kp_ref, vc_ref, vp_ref, o_ref):
    j = pl.program_id(1)
    q = q_ref[...]
    kb = jnp.concatenate([kp_ref[...], kc_ref[...]], axis=0).astype(BF16)
    vb = jnp.concatenate([vp_ref[...], vc_ref[...]], axis=0).astype(BF16)
    r = lax.broadcasted_iota(jnp.int32, (WINDOW, 2 * WINDOW), 0)
    s_idx = lax.broadcasted_iota(jnp.int32, (WINDOW, 2 * WINDOW), 1)
    mask = (s_idx > r) & (s_idx <= r + WINDOW) & ((s_idx >= WINDOW) | (j > 0))
    outs = []
    for kv in range(N_KV_HEADS):
        kk = kb[:, kv * HEAD_DIM:(kv + 1) * HEAD_DIM]
        vv = vb[:, kv * HEAD_DIM:(kv + 1) * HEAD_DIM]
        for g in range(Q_PER_KV):
            h = kv * Q_PER_KV + g
            qh = q[:, h * HEAD_DIM:(h + 1) * HEAD_DIM]
            s = lax.dot_general(qh, kk, (((1,), (1,)), ((), ())), preferred_element_type=F32)
            outs.append(_sink_softmax_pv(s, mask, sink_ref[0, h], vv))
    o_ref[...] = jnp.concatenate(outs, axis=1)


def _prompt_attn(q, k, v, sinks, batch, seq):
    nb = seq // WINDOW
    cur = lambda b, j: (b * nb + j, 0)
    prev = lambda b, j: (b * nb + jnp.maximum(j - 1, 0), 0)
    return pl.pallas_call(
        _prompt_attn_kernel,
        grid=(batch, nb),
        in_specs=[
            pl.BlockSpec(memory_space=pltpu.SMEM),
            pl.BlockSpec((WINDOW, ATTN_WIDTH), cur),
            pl.BlockSpec((WINDOW, KV_WIDTH), cur),
            pl.BlockSpec((WINDOW, KV_WIDTH), prev),
            pl.BlockSpec((WINDOW, KV_WIDTH), cur),
            pl.BlockSpec((WINDOW, KV_WIDTH), prev),
        ],
        out_specs=pl.BlockSpec((WINDOW, ATTN_WIDTH), cur),
        out_shape=jax.ShapeDtypeStruct((batch * seq, ATTN_WIDTH), F32),
        compiler_params=_cparams("parallel", "arbitrary"),
        name="prompt_attn",
    )(sinks, q, k, k, v, v)


SAMPLE_BB = 8
NEW_PAD = 8


def _sample_attn_kernel(sink_ref, q_ref, kn_ref, vn_ref, kc_ref, vc_ref, o_ref, *, dec_seq):
    lc = kc_ref.shape[1]
    rows = dec_seq * Q_PER_KV
    per_pad = NEW_PAD // dec_seq
    t_q = lax.broadcasted_iota(jnp.int32, (rows, lc + NEW_PAD), 0) // Q_PER_KV
    col = lax.broadcasted_iota(jnp.int32, (rows, lc + NEW_PAD), 1)
    g_idx = lax.broadcasted_iota(jnp.int32, (rows, 1), 0) % Q_PER_KV
    for bi in range(SAMPLE_BB):
        lo = (bi % per_pad) * dec_seq
        base = (bi // per_pad) * NEW_PAD
        kall = jnp.concatenate([kc_ref[bi], kn_ref[base:base + NEW_PAD, :]], axis=0).astype(BF16)
        vall = jnp.concatenate([vc_ref[bi], vn_ref[base:base + NEW_PAD, :]], axis=0).astype(BF16)
        new_t = col - lc - lo
        mask = ((col < lc) & (col + WINDOW > t_q + lc)) | ((new_t >= 0) & (new_t <= t_q))
        for kv in range(N_KV_HEADS):
            kk = kall[:, kv * HEAD_DIM:(kv + 1) * HEAD_DIM]
            vv = vall[:, kv * HEAD_DIM:(kv + 1) * HEAD_DIM]
            sink = jnp.zeros((rows, 1), F32)
            for g in range(Q_PER_KV):
                sink = jnp.where(g_idx == g, sink_ref[0, kv * Q_PER_KV + g], sink)
            s = lax.dot_general(q_ref[bi, kv], kk, (((1,), (1,)), ((), ())), preferred_element_type=F32)
            o_ref[bi, kv] = _sink_softmax_pv(s, mask, sink, vv)


def _sample_attn(q, k, v, sinks, cache_k, cache_v, dec_batch, dec_seq):
    lc = cache_k.shape[1]
    rows = dec_seq * Q_PER_KV
    q5 = q.reshape(dec_batch, dec_seq, N_KV_HEADS, Q_PER_KV, HEAD_DIM).transpose(0, 2, 1, 3, 4)
    q4 = q5.reshape(dec_batch, N_KV_HEADS, rows, HEAD_DIM)
    blk_q = pl.BlockSpec((SAMPLE_BB, N_KV_HEADS, rows, HEAD_DIM), lambda i: (i, 0, 0, 0))
    blk_new = pl.BlockSpec((SAMPLE_BB * dec_seq, KV_WIDTH), lambda i: (i, 0))
    blk_cache = pl.BlockSpec((SAMPLE_BB, lc, KV_WIDTH), lambda i: (i, 0, 0))
    o = pl.pallas_call(
        functools.partial(_sample_attn_kernel, dec_seq=dec_seq),
        grid=(dec_batch // SAMPLE_BB,),
        in_specs=[pl.BlockSpec(memory_space=pltpu.SMEM), blk_q, blk_new, blk_new, blk_cache, blk_cache],
        out_specs=blk_q,
        out_shape=jax.ShapeDtypeStruct(q4.shape, F32),
        compiler_params=_cparams("parallel"),
        name="sample_attn",
    )(sinks, q4, k, v, cache_k, cache_v)
    o = o.reshape(dec_batch, N_KV_HEADS, dec_seq, Q_PER_KV, HEAD_DIM).transpose(0, 2, 1, 3, 4)
    return o.reshape(dec_batch * dec_seq, ATTN_WIDTH)


CONV_ROWS = 32


def _conv_taps(ext_ref, w_ref, y_ref, n_out, first_row):
    ch = y_ref.shape[-1]
    for r0 in range(0, n_out, CONV_ROWS):
        nr = min(CONV_ROWS, n_out - r0)
        for c0 in range(0, ch, LANES):
            acc = jnp.zeros((nr, LANES), F32)
            for j in range(CONV_WIDTH):
                acc = acc + ext_ref[pl.ds(first_row + r0 + j, nr), c0:c0 + LANES] * w_ref[j:j + 1, c0:c0 + LANES]
            y_ref[r0:r0 + nr, c0:c0 + LANES] = acc


def _conv_post(y, b_ref, g_ref, beta_ref):
    z = _layer_norm(y + b_ref[...], g_ref[...], beta_ref[...])
    return z * _sigmoid(z)


def _conv_prompt_kernel(cur_ref, halo_ref, w_ref, b_ref, g_ref, beta_ref, o_ref, ext_ref, y_ref):
    i = pl.program_id(1)
    tt = cur_ref.shape[0]
    ext_ref[0:CONV_HALO, :] = jnp.where(i > 0, halo_ref[...], 0.0)
    ext_ref[CONV_HALO:CONV_HALO + tt, :] = cur_ref[...]
    _conv_taps(ext_ref, w_ref, y_ref, tt, CONV_HALO - (CONV_WIDTH - 1))
    o_ref[...] = _conv_post(y_ref[...], b_ref, g_ref, beta_ref)


def _conv_prompt(g, w, b, ln_g, ln_b, batch, seq, tt):
    ch = g.shape[1]
    nt = seq // tt
    hb = tt // CONV_HALO
    vec = pl.BlockSpec((1, ch), lambda bi, i: (0, 0))
    return pl.pallas_call(
        _conv_prompt_kernel,
        grid=(batch, nt),
        in_specs=[
            pl.BlockSpec((tt, ch), lambda bi, i: (bi * nt + i, 0)),
            pl.BlockSpec((CONV_HALO, ch), lambda bi, i: (jnp.maximum((bi * nt + i) * hb - 1, 0), 0)),
            pl.BlockSpec((CONV_WIDTH, ch), lambda bi, i: (0, 0)),
            vec, vec, vec,
        ],
        out_specs=pl.BlockSpec((tt, ch), lambda bi, i: (bi * nt + i, 0)),
        out_shape=jax.ShapeDtypeStruct(g.shape, F32),
        scratch_shapes=[pltpu.VMEM((CONV_HALO + tt, ch), F32), pltpu.VMEM((tt, ch), F32)],
        compiler_params=_cparams("parallel", "arbitrary"),
        name="conv_prompt",
    )(g, g, w, b, ln_g, ln_b)


def _conv_sample_kernel(ext_ref, w_ref, b_ref, g_ref, beta_ref, o_ref, y_ref):
    nb, _, ch = ext_ref.shape
    dec = o_ref.shape[1]
    for bi in range(nb):
        _conv_taps(ext_ref.at[bi], w_ref, y_ref.at[bi], dec, 0)
    for bi in range(nb):
        o_ref[bi] = _conv_post(y_ref[bi], b_ref, g_ref, beta_ref)


def _conv_sample(ext, w, b, ln_g, ln_b, dec_seq):
    nb, rows, ch = ext.shape
    vec = pl.BlockSpec((1, ch), lambda i: (0, 0))
    return pl.pallas_call(
        _conv_sample_kernel,
        grid=(nb // SAMPLE_BB,),
        in_specs=[
            pl.BlockSpec((SAMPLE_BB, rows, ch), lambda i: (i, 0, 0)),
            pl.BlockSpec((CONV_WIDTH, ch), lambda i: (0, 0)),
            vec, vec, vec,
        ],
        out_specs=pl.BlockSpec((SAMPLE_BB, dec_seq, ch), lambda i: (i, 0, 0)),
        out_shape=jax.ShapeDtypeStruct((nb, dec_seq, ch), F32),
        scratch_shapes=[pltpu.VMEM((SAMPLE_BB, dec_seq, ch), F32)],
        compiler_params=_cparams("parallel"),
        name="conv_sample",
    )(ext, w, b, ln_g, ln_b)


def _mix_kernel(attn_ref, conv_ref, x_ref, wo_ref, ag_ref, cg_ref, g1_ref, b1_ref, rw_ref,
                sg_ref, su_ref, sd_ref, base_ref, hb_ref, sc_ref):
    aw = attn_ref.shape[1]
    a = _rms_norm(attn_ref[...], ag_ref[...]).astype(BF16)
    c = _rms_norm(conv_ref[...], cg_ref[...]).astype(BF16)
    mixed = _dot(a, wo_ref[0:aw, :]) + _dot(c, wo_ref[aw:, :])
    h = _layer_norm(DEEPNORM_ALPHA * x_ref[...] + mixed, g1_ref[...], b1_ref[...])
    hb = h.astype(BF16)
    hb_ref[...] = hb
    sc_ref[...] = _sigmoid(_dot(hb, rw_ref[...]))
    h1 = _dot(hb, sg_ref[...])
    act = (h1 * _sigmoid(h1) * _dot(hb, su_ref[...])).astype(BF16)
    base_ref[...] = DEEPNORM_ALPHA * h + _dot(act, sd_ref[...])


def _mix(attn, conv, x2d, wo, ag, cg, g1, b1, rw, sg, su, sd, tm):
    t, d = x2d.shape
    ne = rw.shape[1]
    row = lambda w: pl.BlockSpec((tm, w), lambda i: (i, 0))
    full = lambda a: pl.BlockSpec(a.shape, lambda i: (0,) * a.ndim, pipeline_mode=pl.Buffered(1))
    return pl.pallas_call(
        _mix_kernel,
        grid=(t // tm,),
        in_specs=[row(attn.shape[1]), row(conv.shape[1]), row(d)] + [full(a) for a in (wo, ag, cg, g1, b1, rw, sg, su, sd)],
        out_specs=[row(d), row(d), row(ne)],
        out_shape=[jax.ShapeDtypeStruct((t, d), F32), jax.ShapeDtypeStruct((t, d), BF16),
                   jax.ShapeDtypeStruct((t, ne), F32)],
        compiler_params=_cparams("parallel"),
        name="mix_ln1",
    )(attn, conv, x2d, wo, ag, cg, g1, b1, rw, sg, su, sd)


def _route(scores, router_b):
    t, ne = scores.shape
    per = ne // N_EXPERT_GROUPS
    sel = scores + router_b.astype(F32)
    grp_score = lax.top_k(sel.reshape(t, N_EXPERT_GROUPS, per), 2)[0].sum(-1)
    _, gidx = lax.top_k(grp_score, TOPK_GROUPS)
    gmask = (gidx[:, :, None] == jnp.arange(N_EXPERT_GROUPS)[None, None, :]).any(1)
    emask = jnp.repeat(gmask, per, axis=1)
    _, eidx = lax.top_k(jnp.where(emask, sel, -jnp.inf), TOP_K)
    gates = jnp.take_along_axis(scores, eidx, axis=1)
    gates = gates / gates.sum(-1, keepdims=True) * ROUTED_SCALE
    return eidx.astype(jnp.int32), gates


def _dispatch_plan(eidx, ne, tm):
    t = eidx.shape[0]
    tk = t * TOP_K
    n_tiles = tk // tm + ne
    flat_e = eidx.reshape(-1)
    key = jnp.sort((flat_e << SORT_SHIFT) | jnp.arange(tk, dtype=jnp.int32))
    tok_sorted = (key & ((1 << SORT_SHIFT) - 1)) // TOP_K
    onehot = (eidx[:, :, None] == jnp.arange(ne, dtype=jnp.int32)[None, None, :]).any(1)
    counts = onehot.sum(0).astype(jnp.int32)
    rank = jnp.cumsum(onehot.astype(jnp.int32), axis=0) - 1
    start = jnp.cumsum(counts) - counts
    tiles_e = (counts + tm - 1) // tm
    tile_end = jnp.cumsum(tiles_e)
    tile_start = tile_end - tiles_e
    n_active = tile_end[-1]
    tile_ids = jnp.minimum(jnp.arange(n_tiles, dtype=jnp.int32), n_active - 1)
    tile_expert = jnp.minimum(jnp.searchsorted(tile_end, tile_ids, side='right'), ne - 1).astype(jnp.int32)
    tile_first = (tile_ids == tile_start[tile_expert]).astype(jnp.int32)
    row_e = jnp.repeat(tile_expert, tm)
    local = jnp.arange(n_tiles * tm, dtype=jnp.int32) - tile_start[row_e] * tm
    valid = local < counts[row_e]
    row_tok = jnp.where(valid, tok_sorted[jnp.clip(start[row_e] + local, 0, tk - 1)], 0)
    pos = tile_start[eidx] * tm + jnp.take_along_axis(rank, eidx, axis=1)
    return tile_expert, tile_first, n_active.reshape(1).astype(jnp.int32), row_tok, pos


def _moe_kernel(te_ref, first_ref, nact_ref, x_ref, wg_ref, wu_ref, wd_ref, y_ref, wgb, wub, wdb):
    t = pl.program_id(0)

    @pl.when(t < nact_ref[0])
    def _():
        @pl.when(first_ref[t] == 1)
        def _():
            wgb[...] = wg_ref[...].astype(BF16)
            wub[...] = wu_ref[...].astype(BF16)
            wdb[...] = wd_ref[...].astype(BF16)

        x = x_ref[...]
        h1 = _dot(x, wgb[...])
        act = (h1 * _sigmoid(h1) * _dot(x, wub[...])).astype(BF16)
        y_ref[...] = _dot(act, wdb[...]).astype(BF16)


def _moe(xs, w_gate, w_up, w_down, tile_expert, tile_first, n_active, tm):
    n_rows, d = xs.shape
    ne, _, ff = w_gate.shape
    rows = lambda t, te, fi, na: (jnp.minimum(t, na[0] - 1), 0)
    wmap = lambda t, te, fi, na: (te[t], 0, 0)
    return pl.pallas_call(
        _moe_kernel,
        grid_spec=pltpu.PrefetchScalarGridSpec(
            num_scalar_prefetch=3,
            grid=(n_rows // tm,),
            in_specs=[
                pl.BlockSpec((tm, d), rows),
                pl.BlockSpec((None, d, ff), wmap),
                pl.BlockSpec((None, d, ff), wmap),
                pl.BlockSpec((None, ff, d), wmap),
            ],
            out_specs=pl.BlockSpec((tm, d), rows),
            scratch_shapes=[pltpu.VMEM((d, ff), BF16), pltpu.VMEM((d, ff), BF16), pltpu.VMEM((ff, d), BF16)],
        ),
        out_shape=jax.ShapeDtypeStruct((n_rows, d), BF16),
        compiler_params=_cparams("arbitrary"),
        name="moe_experts",
    )(tile_expert, tile_first, n_active, xs, w_gate, w_up, w_down)


def _combine_kernel(base_ref, y_ref, gate_ref, g_ref, b_ref, o_ref):
    d = base_ref.shape[1]
    f = base_ref[...]
    gates = gate_ref[...]
    for k in range(TOP_K):
        f = f + gates[:, k:k + 1] * y_ref[:, k * d:(k + 1) * d].astype(F32)
    o_ref[...] = _layer_norm(f, g_ref[...], b_ref[...])


def _combine(base, y_tok, gates, ln_g, ln_b, tm):
    t, d = base.shape
    vec = pl.BlockSpec((1, d), lambda i: (0, 0))
    return pl.pallas_call(
        _combine_kernel,
        grid=(t // tm,),
        in_specs=[pl.BlockSpec((tm, d), lambda i: (i, 0)), pl.BlockSpec((tm, TOP_K * d), lambda i: (i, 0)),
                  pl.BlockSpec((tm, TOP_K), lambda i: (i, 0)), vec, vec],
        out_specs=pl.BlockSpec((tm, d), lambda i: (i, 0)),
        out_shape=jax.ShapeDtypeStruct((t, d), F32),
        compiler_params=_cparams("parallel"),
        name="combine_ln2",
    )(base, y_tok, gates, ln_g, ln_b)


def kernel(x_prompt, x_sample, cache_win_k, cache_win_v, state_conv, w_in, attn_sinks, conv_w, conv_b, conv_ln_g, conv_ln_b, attn_out_g, conv_out_g, w_out, ln1_g, ln1_b, router_w, router_b, exp_w_gate, exp_w_up, exp_w_down, sh_w_gate, sh_w_up, sh_w_down, ln2_g, ln2_b):
    assert w_in.shape[0] == DEPTH
    batch, seq, d = x_prompt.shape
    dec_batch, dec_seq, _ = x_sample.shape
    lc = cache_win_k.shape[2]
    ch = conv_w.shape[2]
    ne = router_w.shape[2]
    tp, ts = batch * seq, dec_batch * dec_seq
    row2 = lambda a: a.reshape(1, a.shape[-1])

    w_in_b = w_in.reshape(w_in.shape[1:]).astype(BF16)
    w_out_b = w_out.reshape(w_out.shape[1:]).astype(BF16)
    rw_b = router_w.reshape(router_w.shape[1:]).astype(BF16)
    sg_b = sh_w_gate.reshape(sh_w_gate.shape[1:]).astype(BF16)
    su_b = sh_w_up.reshape(sh_w_up.shape[1:]).astype(BF16)
    sd_b = sh_w_down.reshape(sh_w_down.shape[1:]).astype(BF16)
    sinks = attn_sinks.reshape(1, N_Q_HEADS)
    cw = conv_w.reshape(CONV_WIDTH, ch)
    cb, clg, clb = row2(conv_b), row2(conv_ln_g), row2(conv_ln_b)

    xp = x_prompt.reshape(tp, d)
    xs_in = x_sample.reshape(ts, d)
    cos_p, sin_p = _rope_tables(jnp.arange(seq))
    cos_s, sin_s = _rope_tables(jnp.tile(PAST_LEN + jnp.arange(dec_seq), dec_batch))

    assert NEW_PAD % dec_seq == 0 and dec_batch % SAMPLE_BB == 0 and seq % WINDOW == 0
    tm_p, tm_s = min(ROW_TILE, seq), min(ROW_TILE, ts)
    qp, kp, vp, gp = _in_proj(xp, w_in_b, cos_p, sin_p, tm_p)
    attn_p = _prompt_attn(qp, kp, vp, sinks, batch, seq)
    conv_p = _conv_prompt(gp, cw, cb, clg, clb, batch, seq, WINDOW)
    qs, ks, vs, gs = _in_proj(xs_in, w_in_b, cos_s, sin_s, tm_s)
    ck = cache_win_k.reshape(dec_batch, lc, KV_WIDTH)
    cv = cache_win_v.reshape(dec_batch, lc, KV_WIDTH)
    attn_s = _sample_attn(qs, ks, vs, sinks, ck, cv, dec_batch, dec_seq)
    ext_s = jnp.concatenate([state_conv.reshape(dec_batch, CONV_WIDTH - 1, ch), gs.reshape(dec_batch, dec_seq, ch)], axis=1)
    conv_s = _conv_sample(ext_s, cw, cb, clg, clb, dec_seq).reshape(ts, ch)

    mix_w = (w_out_b, row2(attn_out_g), row2(conv_out_g), row2(ln1_g), row2(ln1_b), rw_b, sg_b, su_b, sd_b)
    base_p, hb_p, sc_p = _mix(attn_p, conv_p, xp, *mix_w, tm_p // 2)
    base_s, hb_s, sc_s = _mix(attn_s, conv_s, xs_in, *mix_w, tm_s // 2)

    base = jnp.concatenate([base_p, base_s], axis=0)
    hb = jnp.concatenate([hb_p, hb_s], axis=0)
    scores = jnp.concatenate([sc_p, sc_s], axis=0)
    eidx, gates = _route(scores, router_b.reshape(ne))
    tile_expert, tile_first, n_active, row_tok, pos = _dispatch_plan(eidx, ne, EXPERT_TILE)
    xs = hb.at[row_tok].get(mode="promise_in_bounds")
    wg = exp_w_gate.reshape(exp_w_gate.shape[1:])
    wu = exp_w_up.reshape(exp_w_up.shape[1:])
    wd = exp_w_down.reshape(exp_w_down.shape[1:])
    y_sorted = _moe(xs, wg, wu, wd, tile_expert, tile_first, n_active, EXPERT_TILE)
    y_tok = y_sorted.at[pos.reshape(-1)].get(mode="promise_in_bounds").reshape(tp + ts, TOP_K * d)
    y = _combine(base, y_tok, gates, row2(ln2_g), row2(ln2_b), min(tm_p, tm_s) // 2)

    y_p = y[:tp].reshape(batch, seq, d)
    y_s = y[tp:].reshape(dec_batch, dec_seq, d)
    keep = min(WINDOW, seq)
    kv5 = lambda a, b: a.reshape(1, b, -1, N_KV_HEADS, HEAD_DIM)
    new_k_p = kv5(kp.reshape(batch, seq, KV_WIDTH)[:, seq - keep:], batch)
    new_v_p = kv5(vp.reshape(batch, seq, KV_WIDTH)[:, seq - keep:], batch)
    new_c_p = gp.reshape(batch, seq, ch)[:, seq - (CONV_WIDTH - 1):].reshape(1, batch, CONV_WIDTH - 1, ch)
    keep_s = min(WINDOW, PAST_LEN + dec_seq)
    k_all = jnp.concatenate([ck, ks.reshape(dec_batch, dec_seq, KV_WIDTH)], axis=1)
    v_all = jnp.concatenate([cv, vs.reshape(dec_batch, dec_seq, KV_WIDTH)], axis=1)
    new_k_s = kv5(k_all[:, k_all.shape[1] - keep_s:], dec_batch)
    new_v_s = kv5(v_all[:, v_all.shape[1] - keep_s:], dec_batch)
    new_c_s = ext_s[:, dec_seq:].reshape(1, dec_batch, CONV_WIDTH - 1, ch)
    return (y_p, y_s, new_k_p, new_v_p, new_c_p, new_k_s, new_v_s, new_c_s)
```

```python
import functools

import jax
import jax.numpy as jnp
from jax import lax
from jax.experimental import pallas as pl
from jax.experimental.pallas import tpu as pltpu

F32 = jnp.float32
BF16 = jnp.bfloat16

DEPTH = 1
PAST_LEN = 16384
HEAD_DIM = 64
N_Q_HEADS = 16
N_KV_HEADS = 4
Q_PER_KV = N_Q_HEADS // N_KV_HEADS
ATTN_WIDTH = N_Q_HEADS * HEAD_DIM
KV_WIDTH = N_KV_HEADS * HEAD_DIM
WINDOW = 128
ROPE_THETA = 10000.0
CONV_WIDTH = 31
CONV_HALO = 32
TOP_K = 8
N_EXPERT_GROUPS = 8
TOPK_GROUPS = 4
ROUTED_SCALE = 2.5
LN_EPS = 1e-5
RMS_EPS = 1e-6
DEEPNORM_ALPHA = (2.0 * DEPTH) ** 0.25

LANES = 128
VMEM_LIMIT = 56 * 1024 * 1024
ROW_TILE = 512
EXPERT_TILE = 256
PACK_ROWS = 8
ROUTE_TILE = 512
DMA_UNROLL = 8


def _cparams(*sem):
    return pltpu.CompilerParams(dimension_semantics=sem, vmem_limit_bytes=VMEM_LIMIT)


def _dot(a, b):
    return jnp.dot(a, b, preferred_element_type=F32)


def _layer_norm(x, g, b):
    xc = x - jnp.mean(x, axis=-1, keepdims=True)
    var = jnp.mean(xc * xc, axis=-1, keepdims=True)
    return xc * lax.rsqrt(var + LN_EPS) * g + b


def _rms_norm(x, g):
    ms = jnp.mean(x * x, axis=-1, keepdims=True)
    return x * lax.rsqrt(ms + RMS_EPS) * g


def _sigmoid(x):
    return 1.0 / (1.0 + jnp.exp(-x))


def _rope_tables(pos):
    half = HEAD_DIM // 2
    inv_freq = 1.0 / (ROPE_THETA ** (jnp.arange(half, dtype=F32) * (2.0 / HEAD_DIM)))
    ang = pos.astype(F32)[:, None] * inv_freq[None, :]
    cos, sin = jnp.cos(ang), jnp.sin(ang)
    reps = LANES // HEAD_DIM
    return jnp.tile(cos, (1, 2 * reps)), jnp.tile(jnp.concatenate([-sin, sin], axis=1), (1, reps))


def _in_proj_kernel(x_ref, w_ref, cos_ref, sin_ref, q_ref, k_ref, v_ref, g_ref):
    xb = x_ref[...].astype(BF16)
    cos = cos_ref[...]
    sin = sin_ref[...]
    lane = lax.broadcasted_iota(jnp.int32, cos.shape, 1)
    first_half = (lane % HEAD_DIM) < (HEAD_DIM // 2)

    def rope(acc):
        outs = []
        for c in range(acc.shape[1] // LANES):
            a = acc[:, c * LANES:(c + 1) * LANES]
            partner = jnp.where(first_half, pltpu.roll(a, LANES - HEAD_DIM // 2, 1),
                                pltpu.roll(a, HEAD_DIM // 2, 1))
            outs.append(a * cos + partner * sin)
        return jnp.concatenate(outs, axis=1)

    o_k = ATTN_WIDTH
    o_v = o_k + KV_WIDTH
    o_a = o_v + KV_WIDTH
    ch = (w_ref.shape[1] - o_a) // 2
    q = rope(_dot(xb, w_ref[:, 0:o_k]))
    q_ref[...] = (q * (HEAD_DIM ** -0.5)).astype(BF16)
    k_ref[...] = rope(_dot(xb, w_ref[:, o_k:o_v]))
    v_ref[...] = _dot(xb, w_ref[:, o_v:o_a])
    a = _dot(xb, w_ref[:, o_a:o_a + ch])
    gate = _dot(xb, w_ref[:, o_a + ch:o_a + 2 * ch])
    g_ref[...] = a * _sigmoid(gate)


def _in_proj(x2d, w_bf16, cos, sin, tm):
    t, d = x2d.shape
    n = w_bf16.shape[1]
    ch = (n - ATTN_WIDTH - 2 * KV_WIDTH) // 2
    ntab = cos.shape[0] // tm
    return pl.pallas_call(
        _in_proj_kernel,
        grid=(t // tm,),
        in_specs=[
            pl.BlockSpec((tm, d), lambda i: (i, 0)),
            pl.BlockSpec((d, n), lambda i: (0, 0), pipeline_mode=pl.Buffered(1)),
            pl.BlockSpec((tm, LANES), lambda i: (i % ntab, 0)),
            pl.BlockSpec((tm, LANES), lambda i: (i % ntab, 0)),
        ],
        out_specs=[
            pl.BlockSpec((tm, ATTN_WIDTH), lambda i: (i, 0)),
            pl.BlockSpec((tm, KV_WIDTH), lambda i: (i, 0)),
            pl.BlockSpec((tm, KV_WIDTH), lambda i: (i, 0)),
            pl.BlockSpec((tm, ch), lambda i: (i, 0)),
        ],
        out_shape=[
            jax.ShapeDtypeStruct((t, ATTN_WIDTH), BF16),
            jax.ShapeDtypeStruct((t, KV_WIDTH), F32),
            jax.ShapeDtypeStruct((t, KV_WIDTH), F32),
            jax.ShapeDtypeStruct((t, ch), F32),
        ],
        compiler_params=_cparams("parallel"),
        name="in_proj",
    )(x2d, w_bf16, cos, sin)


def _sink_softmax_pv(s, mask, sink, v_bf16):
    s = jnp.where(mask, s, -jnp.inf)
    m = jnp.maximum(jnp.max(s, axis=-1, keepdims=True), sink)
    e = jnp.exp(s - m)
    den = jnp.sum(e, axis=-1, keepdims=True) + jnp.exp(sink - m)
    return _dot(e.astype(BF16), v_bf16) / den


def _prompt_attn_kernel(sink_ref, q_ref, kc_ref, kp_ref, vc_ref, vp_ref, o_ref):
    j = pl.program_id(1)
    q = q_ref[...]
    kb = jnp.concatenate([kp_ref[...], kc_ref[...]], axis=0).astype(BF16)
    vb = jnp.concatenate([vp_ref[...], vc_ref[...]], axis=0).astype(BF16)
    r = lax.broadcasted_iota(jnp.int32, (WINDOW, 2 * WINDOW), 0)
    s_idx = lax.broadcasted_iota(jnp.int32, (WINDOW, 2 * WINDOW), 1)
    mask = (s_idx > r) & (s_idx <= r + WINDOW) & ((s_idx >= WINDOW) | (j > 0))
    outs = []
    for kv in range(N_KV_HEADS):
        kk = kb[:, kv * HEAD_DIM:(kv + 1) * HEAD_DIM]
        vv = vb[:, kv * HEAD_DIM:(kv + 1) * HEAD_DIM]
        for g in range(Q_PER_KV):
            h = kv * Q_PER_KV + g
            qh = q[:, h * HEAD_DIM:(h + 1) * HEAD_DIM]
            s = lax.dot_general(qh, kk, (((1,), (1,)), ((), ())), preferred_element_type=F32)
            outs.append(_sink_softmax_pv(s, mask, sink_ref[0, h], vv))
    o_ref[...] = jnp.concatenate(outs, axis=1)


def _prompt_attn(q, k, v, sinks, batch, seq):
    nb = seq // WINDOW
    cur = lambda b, j: (b * nb + j, 0)
    prev = lambda b, j: (b * nb + jnp.maximum(j - 1, 0), 0)
    return pl.pallas_call(
        _prompt_attn_kernel,
        grid=(batch, nb),
        in_specs=[
            pl.BlockSpec(memory_space=pltpu.SMEM),
            pl.BlockSpec((WINDOW, ATTN_WIDTH), cur),
            pl.BlockSpec((WINDOW, KV_WIDTH), cur),
            pl.BlockSpec((WINDOW, KV_WIDTH), prev),
            pl.BlockSpec((WINDOW, KV_WIDTH), cur),
            pl.BlockSpec((WINDOW, KV_WIDTH), prev),
        ],
        out_specs=pl.BlockSpec((WINDOW, ATTN_WIDTH), cur),
        out_shape=jax.ShapeDtypeStruct((batch * seq, ATTN_WIDTH), F32),
        compiler_params=_cparams("parallel", "arbitrary"),
        name="prompt_attn",
    )(sinks, q, k, k, v, v)


SAMPLE_BB = 8
NEW_PAD = 8


def _sample_attn_kernel(sink_ref, q_ref, kn_ref, vn_ref, kc_ref, vc_ref, o_ref, *, dec_seq):
    lc = kc_ref.shape[1]
    rows = dec_seq * Q_PER_KV
    per_pad = NEW_PAD // dec_seq
    t_q = lax.broadcasted_iota(jnp.int32, (rows, lc + NEW_PAD), 0) // Q_PER_KV
    col = lax.broadcasted_iota(jnp.int32, (rows, lc + NEW_PAD), 1)
    g_idx = lax.broadcasted_iota(jnp.int32, (rows, 1), 0) % Q_PER_KV
    for bi in range(SAMPLE_BB):
        lo = (bi % per_pad) * dec_seq
        base = (bi // per_pad) * NEW_PAD
        kall = jnp.concatenate([kc_ref[bi], kn_ref[base:base + NEW_PAD, :]], axis=0).astype(BF16)
        vall = jnp.concatenate([vc_ref[bi], vn_ref[base:base + NEW_PAD, :]], axis=0).astype(BF16)
        new_t = col - lc - lo
        mask = ((col < lc) & (col + WINDOW > t_q + lc)) | ((new_t >= 0) & (new_t <= t_q))
        for kv in range(N_KV_HEADS):
            kk = kall[:, kv * HEAD_DIM:(kv + 1) * HEAD_DIM]
            vv = vall[:, kv * HEAD_DIM:(kv + 1) * HEAD_DIM]
            sink = jnp.zeros((rows, 1), F32)
            for g in range(Q_PER_KV):
                sink = jnp.where(g_idx == g, sink_ref[0, kv * Q_PER_KV + g], sink)
            s = lax.dot_general(q_ref[bi, kv], kk, (((1,), (1,)), ((), ())), preferred_element_type=F32)
            o_ref[bi, kv] = _sink_softmax_pv(s, mask, sink, vv)


def _sample_attn(q, k, v, sinks, cache_k, cache_v, dec_batch, dec_seq):
    lc = cache_k.shape[1]
    rows = dec_seq * Q_PER_KV
    q5 = q.reshape(dec_batch, dec_seq, N_KV_HEADS, Q_PER_KV, HEAD_DIM).transpose(0, 2, 1, 3, 4)
    q4 = q5.reshape(dec_batch, N_KV_HEADS, rows, HEAD_DIM)
    blk_q = pl.BlockSpec((SAMPLE_BB, N_KV_HEADS, rows, HEAD_DIM), lambda i: (i, 0, 0, 0))
    blk_new = pl.BlockSpec((SAMPLE_BB * dec_seq, KV_WIDTH), lambda i: (i, 0))
    blk_cache = pl.BlockSpec((SAMPLE_BB, lc, KV_WIDTH), lambda i: (i, 0, 0))
    o = pl.pallas_call(
        functools.partial(_sample_attn_kernel, dec_seq=dec_seq),
        grid=(dec_batch // SAMPLE_BB,),
        in_specs=[pl.BlockSpec(memory_space=pltpu.SMEM), blk_q, blk_new, blk_new, blk_cache, blk_cache],
        out_specs=blk_q,
        out_shape=jax.ShapeDtypeStruct(q4.shape, F32),
        compiler_params=_cparams("parallel"),
        name="sample_attn",
    )(sinks, q4, k, v, cache_k, cache_v)
    o = o.reshape(dec_batch, N_KV_HEADS, dec_seq, Q_PER_KV, HEAD_DIM).transpose(0, 2, 1, 3, 4)
    return o.reshape(dec_batch * dec_seq, ATTN_WIDTH)


CONV_ROWS = 32


def _conv_taps(ext_ref, w_ref, y_ref, n_out, first_row):
    ch = y_ref.shape[-1]
    for r0 in range(0, n_out, CONV_ROWS):
        nr = min(CONV_ROWS, n_out - r0)
        for c0 in range(0, ch, LANES):
            acc = jnp.zeros((nr, LANES), F32)
            for j in range(CONV_WIDTH):
                acc = acc + ext_ref[pl.ds(first_row + r0 + j, nr), c0:c0 + LANES] * w_ref[j:j + 1, c0:c0 + LANES]
            y_ref[r0:r0 + nr, c0:c0 + LANES] = acc


def _conv_post(y, b_ref, g_ref, beta_ref):
    z = _layer_norm(y + b_ref[...], g_ref[...], beta_ref[...])
    return z * _sigmoid(z)


def _conv_prompt_kernel(cur_ref, halo_ref, w_ref, b_ref, g_ref, beta_ref, o_ref, ext_ref, y_ref):
    i = pl.program_id(1)
    tt = cur_ref.shape[0]
    ext_ref[0:CONV_HALO, :] = jnp.where(i > 0, halo_ref[...], 0.0)
    ext_ref[CONV_HALO:CONV_HALO + tt, :] = cur_ref[...]
    _conv_taps(ext_ref, w_ref, y_ref, tt, CONV_HALO - (CONV_WIDTH - 1))
    o_ref[...] = _conv_post(y_ref[...], b_ref, g_ref, beta_ref)


def _conv_prompt(g, w, b, ln_g, ln_b, batch, seq, tt):
    ch = g.shape[1]
    nt = seq // tt
    hb = tt // CONV_HALO
    vec = pl.BlockSpec((1, ch), lambda bi, i: (0, 0))
    return pl.pallas_call(
        _conv_prompt_kernel,
        grid=(batch, nt),
        in_specs=[
            pl.BlockSpec((tt, ch), lambda bi, i: (bi * nt + i, 0)),
            pl.BlockSpec((CONV_HALO, ch), lambda bi, i: (jnp.maximum((bi * nt + i) * hb - 1, 0), 0)),
            pl.BlockSpec((CONV_WIDTH, ch), lambda bi, i: (0, 0)),
            vec, vec, vec,
        ],
        out_specs=pl.BlockSpec((tt, ch), lambda bi, i: (bi * nt + i, 0)),
        out_shape=jax.ShapeDtypeStruct(g.shape, F32),
        scratch_shapes=[pltpu.VMEM((CONV_HALO + tt, ch), F32), pltpu.VMEM((tt, ch), F32)],
        compiler_params=_cparams("parallel", "arbitrary"),
        name="conv_prompt",
    )(g, g, w, b, ln_g, ln_b)


def _conv_sample_kernel(ext_ref, w_ref, b_ref, g_ref, beta_ref, o_ref, y_ref):
    nb, _, ch = ext_ref.shape
    dec = o_ref.shape[1]
    for bi in range(nb):
        _conv_taps(ext_ref.at[bi], w_ref, y_ref.at[bi], dec, 0)
    for bi in range(nb):
        o_ref[bi] = _conv_post(y_ref[bi], b_ref, g_ref, beta_ref)


def _conv_sample(ext, w, b, ln_g, ln_b, dec_seq):
    nb, rows, ch = ext.shape
    vec = pl.BlockSpec((1, ch), lambda i: (0, 0))
    return pl.pallas_call(
        _conv_sample_kernel,
        grid=(nb // SAMPLE_BB,),
        in_specs=[
            pl.BlockSpec((SAMPLE_BB, rows, ch), lambda i: (i, 0, 0)),
            pl.BlockSpec((CONV_WIDTH, ch), lambda i: (0, 0)),
            vec, vec, vec,
        ],
        out_specs=pl.BlockSpec((SAMPLE_BB, dec_seq, ch), lambda i: (i, 0, 0)),
        out_shape=jax.ShapeDtypeStruct((nb, dec_seq, ch), F32),
        scratch_shapes=[pltpu.VMEM((SAMPLE_BB, dec_seq, ch), F32)],
        compiler_params=_cparams("parallel"),
        name="conv_sample",
    )(ext, w, b, ln_g, ln_b)


def _bf16_words(vals):
    half = vals.shape[1] // 2
    bits = lax.bitcast_convert_type(vals, jnp.uint32)
    return (bits[:, :half] >> 16) | (bits[:, half:] & jnp.uint32(0xFFFF0000))


def _word_halves(words):
    lo = lax.bitcast_convert_type(words << 16, F32)
    hi = lax.bitcast_convert_type(words & jnp.uint32(0xFFFF0000), F32)
    return lo, hi


def _store_packed(ref, words, base=0):
    n = words.shape[0]
    for s in range(PACK_ROWS):
        ref[pl.ds(base + s, n, stride=PACK_ROWS), :] = words[:, s * LANES:(s + 1) * LANES]


def _mix_kernel(ap_ref, as_ref, cp_ref, cs_ref, xp_ref, xs_ref, wo_ref, ag_ref, cg_ref, g1_ref, b1_ref, rwt_ref,
                sg_ref, su_ref, sd_ref, base_ref, hbp_ref, sct_ref, *, n_prompt_tiles):
    is_p = pl.program_id(0) < n_prompt_tiles
    attn = jnp.where(is_p, ap_ref[...], as_ref[...])
    conv = jnp.where(is_p, cp_ref[...], cs_ref[...])
    x = jnp.where(is_p, xp_ref[...], xs_ref[...])
    aw = attn.shape[1]
    a = _rms_norm(attn, ag_ref[...]).astype(BF16)
    c = _rms_norm(conv, cg_ref[...]).astype(BF16)
    mixed = _dot(a, wo_ref[0:aw, :]) + _dot(c, wo_ref[aw:, :])
    h = _layer_norm(DEEPNORM_ALPHA * x + mixed, g1_ref[...], b1_ref[...])
    hb = h.astype(BF16)
    _store_packed(hbp_ref, _bf16_words(hb.astype(F32)))
    logits_t = lax.dot_general(rwt_ref[...], hb, (((1,), (1,)), ((), ())), preferred_element_type=F32)
    sct_ref[...] = _sigmoid(logits_t)
    h1 = _dot(hb, sg_ref[...])
    act = (h1 * _sigmoid(h1) * _dot(hb, su_ref[...])).astype(BF16)
    base_ref[...] = DEEPNORM_ALPHA * h + _dot(act, sd_ref[...])


def _mix(attn_p, attn_s, conv_p, conv_s, xp, xs, wo, ag, cg, g1, b1, rwt, sg, su, sd, tm):
    tp, d = xp.shape
    ts = xs.shape[0]
    n_p, n_s = tp // tm, ts // tm
    ne = rwt.shape[0]
    t = tp + ts
    prow = lambda w: pl.BlockSpec((tm, w), lambda i: (jnp.minimum(i, n_p - 1), 0))
    srow = lambda w: pl.BlockSpec((tm, w), lambda i: (jnp.maximum(i - n_p, 0), 0))
    full = lambda a: pl.BlockSpec(a.shape, lambda i: (0,) * a.ndim, pipeline_mode=pl.Buffered(1))
    aw, cw = attn_p.shape[1], conv_p.shape[1]
    return pl.pallas_call(
        functools.partial(_mix_kernel, n_prompt_tiles=n_p),
        grid=(n_p + n_s,),
        in_specs=[prow(aw), srow(aw), prow(cw), srow(cw), prow(d), srow(d)]
        + [full(a) for a in (wo, ag, cg, g1, b1, rwt, sg, su, sd)],
        out_specs=[pl.BlockSpec((tm, d), lambda i: (i, 0)),
                   pl.BlockSpec((tm * PACK_ROWS, LANES), lambda i: (i, 0)),
                   pl.BlockSpec((ne, tm), lambda i: (0, i))],
        out_shape=[jax.ShapeDtypeStruct((t, d), F32),
                   jax.ShapeDtypeStruct((t * PACK_ROWS, LANES), jnp.uint32),
                   jax.ShapeDtypeStruct((ne, t), F32)],
        compiler_params=_cparams("arbitrary"),
        name="mix_ln1",
    )(attn_p, attn_s, conv_p, conv_s, xp, xs, wo, ag, cg, g1, b1, rwt, sg, su, sd)


def _route_kernel(sc_ref, bias_ref, tri_ref, eidx_ref, gate_ref, rank_ref, cnt_ref, carry_ref, mask_ref, eidf_ref):
    ne, tn = sc_ref.shape
    per = ne // N_EXPERT_GROUPS
    ninf = -jnp.inf

    @pl.when(pl.program_id(0) == 0)
    def _():
        carry_ref[...] = jnp.zeros_like(carry_ref)

    erow = lax.broadcasted_iota(jnp.int32, (ne, LANES), 0).astype(F32)
    grow = lax.broadcasted_iota(jnp.int32, (per, LANES), 0).astype(F32)
    gi = lax.broadcasted_iota(jnp.int32, (N_EXPERT_GROUPS, LANES), 0).astype(F32)
    colmax = lambda v: jnp.max(v, axis=0, keepdims=True)
    first_at = lambda v, m, idx, n: jnp.min(jnp.where(v == m, idx, float(n)), axis=0, keepdims=True)

    for c in range(tn // LANES):
        lanes = slice(c * LANES, (c + 1) * LANES)
        sc = sc_ref[:, lanes]
        sel = sc + bias_ref[...]
        gs = []
        for g in range(N_EXPERT_GROUPS):
            v = sel[g * per:(g + 1) * per, :]
            m1 = colmax(v)
            m2 = colmax(jnp.where(grow == first_at(v, m1, grow, per), ninf, v))
            gs.append(m1 + m2)
        grp = jnp.concatenate(gs, axis=0)
        gsel = jnp.zeros_like(grp)
        for _ in range(TOPK_GROUPS):
            m = colmax(grp)
            hit = gi == first_at(grp, m, gi, N_EXPERT_GROUPS)
            gsel = jnp.where(hit, 1.0, gsel)
            grp = jnp.where(hit, ninf, grp)
        emask = jnp.concatenate([jnp.broadcast_to(gsel[g:g + 1, :], (per, LANES)) for g in range(N_EXPERT_GROUPS)], axis=0)
        masked = jnp.where(emask > 0.5, sel, ninf)
        selm = jnp.zeros_like(sel)
        idxs, gts = [], []
        for _ in range(TOP_K):
            ix = first_at(masked, colmax(masked), erow, ne)
            hit = erow == ix
            gts.append(jnp.sum(jnp.where(hit, sc, 0.0), axis=0, keepdims=True))
            idxs.append(ix)
            masked = jnp.where(hit, ninf, masked)
            selm = jnp.where(hit, 1.0, selm)
        gsum = gts[0]
        for g in gts[1:]:
            gsum = gsum + g
        eidf = jnp.concatenate(idxs, axis=0)
        eidf_ref[:, lanes] = eidf
        eidx_ref[:, lanes] = eidf.astype(jnp.int32)
        gate_ref[:, lanes] = jnp.concatenate([g / gsum * ROUTED_SCALE for g in gts], axis=0)
        mask_ref[:, lanes] = selm

    selm_all = mask_ref[...]
    incl = _dot(selm_all.astype(BF16), tri_ref[...])
    carry = carry_ref[...]
    for c in range(tn // LANES):
        lanes = slice(c * LANES, (c + 1) * LANES)
        before = incl[:, lanes] - selm_all[:, lanes] + carry
        eidf = eidf_ref[:, lanes]
        rk = [jnp.sum(jnp.where(erow == eidf[k:k + 1, :], before, 0.0), axis=0, keepdims=True) for k in range(TOP_K)]
        rank_ref[:, lanes] = jnp.concatenate(rk, axis=0).astype(jnp.int32)
    total = carry + jnp.broadcast_to(incl[:, tn - 1:tn], (ne, LANES))
    carry_ref[...] = total
    cnt_ref[...] = total.astype(jnp.int32)


def _route(sct, bias_l, tn):
    ne, t = sct.shape
    tri = jnp.triu(jnp.ones((tn, tn), BF16))
    col = pl.BlockSpec((TOP_K, tn), lambda i: (0, i))
    return pl.pallas_call(
        _route_kernel,
        grid=(t // tn,),
        in_specs=[pl.BlockSpec((ne, tn), lambda i: (0, i)),
                  pl.BlockSpec((ne, LANES), lambda i: (0, 0)),
                  pl.BlockSpec((tn, tn), lambda i: (0, 0))],
        out_specs=[col, col, col, pl.BlockSpec((ne, LANES), lambda i: (0, 0))],
        out_shape=[jax.ShapeDtypeStruct((TOP_K, t), jnp.int32), jax.ShapeDtypeStruct((TOP_K, t), F32),
                   jax.ShapeDtypeStruct((TOP_K, t), jnp.int32), jax.ShapeDtypeStruct((ne, LANES), jnp.int32)],
        scratch_shapes=[pltpu.VMEM((ne, LANES), F32), pltpu.VMEM((ne, tn), F32), pltpu.VMEM((TOP_K, tn), F32)],
        compiler_params=_cparams("arbitrary"),
        name="route",
    )(sct, bias_l, tri)


def _pos_kernel(pstart_ref, eidx_ref, rank_ref, pos_ref):
    eid = eidx_ref[...]
    base = lax.fori_loop(0, pstart_ref.shape[0], lambda e, acc: jnp.where(eid == e, pstart_ref[e], acc),
                         jnp.zeros_like(eid))
    pos_ref[...] = base + rank_ref[...]


def _positions(pstart, eidx_t, rank_t, tm):
    t = eidx_t.shape[1]
    col = pl.BlockSpec((TOP_K, tm), lambda i, ps: (0, i))
    return pl.pallas_call(
        _pos_kernel,
        grid_spec=pltpu.PrefetchScalarGridSpec(
            num_scalar_prefetch=1, grid=(t // tm,), in_specs=[col, col],
            out_specs=pl.BlockSpec((None, TOP_K, tm), lambda i, ps: (i, 0, 0))),
        out_shape=jax.ShapeDtypeStruct((t // tm, TOP_K, tm), jnp.int32),
        compiler_params=_cparams("parallel"),
        name="positions",
    )(pstart, eidx_t, rank_t)


def _packed_rows(ref, row):
    return ref.at[pl.ds(pl.multiple_of(row * PACK_ROWS, PACK_ROWS), PACK_ROWS), :]


def _dispatch_kernel(last_ref, pos_hbm, hb_ref, xs_hbm, pos_s, zero_ref, sem_pos, sem_zero, sem_row):
    i = pl.program_id(0)
    tm = pos_s.shape[1]
    tile_rows = zero_ref.shape[0]

    @pl.when(i == 0)
    def _():
        zero_ref[...] = jnp.zeros_like(zero_ref)

        def fill(e):
            start = pl.multiple_of(last_ref[e] * PACK_ROWS, PACK_ROWS)
            return pltpu.make_async_copy(zero_ref, xs_hbm.at[pl.ds(start, tile_rows), :], sem_zero)

        def go(e, c):
            @pl.when(last_ref[e] >= 0)
            def _():
                fill(e).start()
            return c

        def done(e, c):
            @pl.when(last_ref[e] >= 0)
            def _():
                fill(e).wait()
            return c

        def tail(t):
            start = pl.multiple_of(t * tile_rows, PACK_ROWS)
            return pltpu.make_async_copy(zero_ref, xs_hbm.at[pl.ds(start, tile_rows), :], sem_zero)

        def tail_go(t, c):
            tail(t).start()
            return c

        def tail_done(t, c):
            tail(t).wait()
            return c

        n_experts = last_ref.shape[0] - 1
        n_active, n_tiles = last_ref[n_experts], xs_hbm.shape[0] // tile_rows
        lax.fori_loop(0, n_experts, go, 0)
        lax.fori_loop(n_active, n_tiles, tail_go, 0)
        lax.fori_loop(0, n_experts, done, 0)
        lax.fori_loop(n_active, n_tiles, tail_done, 0)

    get_pos = pltpu.make_async_copy(pos_hbm.at[i], pos_s, sem_pos)
    get_pos.start()
    get_pos.wait()

    def row_copy(r, k):
        return pltpu.make_async_copy(_packed_rows(hb_ref, r), _packed_rows(xs_hbm, pos_s[k, r]), sem_row)

    def issue(r, c):
        for k in range(TOP_K):
            row_copy(r, k).start()
        return c

    def drain(r, c):
        for k in range(TOP_K):
            row_copy(r, k).wait()
        return c

    lax.fori_loop(0, tm, issue, 0, unroll=DMA_UNROLL)
    lax.fori_loop(0, tm, drain, 0, unroll=DMA_UNROLL)


def _dispatch(fill_plan, pos3, hbp, n_rows, tm, expert_tile):
    n_tok_tiles = pos3.shape[0]
    return pl.pallas_call(
        _dispatch_kernel,
        grid_spec=pltpu.PrefetchScalarGridSpec(
            num_scalar_prefetch=1, grid=(n_tok_tiles,),
            in_specs=[pl.BlockSpec(memory_space=pl.ANY),
                      pl.BlockSpec((tm * PACK_ROWS, LANES), lambda i, lt: (i, 0))],
            out_specs=pl.BlockSpec(memory_space=pl.ANY),
            scratch_shapes=[pltpu.SMEM((TOP_K, tm), jnp.int32),
                            pltpu.VMEM((expert_tile * PACK_ROWS, LANES), jnp.uint32),
                            pltpu.SemaphoreType.DMA(()), pltpu.SemaphoreType.DMA(()), pltpu.SemaphoreType.DMA(())]),
        out_shape=jax.ShapeDtypeStruct((n_rows * PACK_ROWS, LANES), jnp.uint32),
        compiler_params=_cparams("arbitrary"),
        name="dispatch",
    )(fill_plan, pos3, hbp)


def _moe_kernel(first_ref, count_ref, x_hbm, wg_ref, wu_ref, wd_ref, y_hbm, xbuf, ybuf, wgb, wub, wdb, sem_x, sem_y):
    e = pl.program_id(0)
    n, t0 = count_ref[e], first_ref[e]
    tile_rows = xbuf.shape[0] // 2
    tm = tile_rows // PACK_ROWS

    def hbm_tile(ref, i):
        return ref.at[pl.ds(pl.multiple_of((t0 + i) * tile_rows, PACK_ROWS), tile_rows), :]

    def buf_tile(ref, slot):
        return ref.at[pl.ds(pl.multiple_of(slot * tile_rows, PACK_ROWS), tile_rows), :]

    def x_copy(i, slot):
        return pltpu.make_async_copy(hbm_tile(x_hbm, i), buf_tile(xbuf, slot), sem_x.at[slot])

    def y_copy(i, slot):
        return pltpu.make_async_copy(buf_tile(ybuf, slot), hbm_tile(y_hbm, i), sem_y.at[slot])

    @pl.when(n > 0)
    def _():
        x_copy(0, 0).start()
        wgb[...] = wg_ref[...].astype(BF16)
        wub[...] = wu_ref[...].astype(BF16)
        wdb[...] = wd_ref[...].astype(BF16)

        def step(i, c):
            slot = i % 2
            base = slot * tile_rows
            x_copy(i, slot).wait()

            @pl.when(i + 1 < n)
            def _():
                x_copy(i + 1, 1 - slot).start()

            @pl.when(i >= 2)
            def _():
                y_copy(i - 2, slot).wait()

            halves = [_word_halves(xbuf[pl.ds(base + s, tm, stride=PACK_ROWS), :]) for s in range(PACK_ROWS)]
            x = jnp.concatenate([h[0] for h in halves] + [h[1] for h in halves], axis=1).astype(BF16)
            h1 = _dot(x, wgb[...])
            act = (h1 * _sigmoid(h1) * _dot(x, wub[...])).astype(BF16)
            y = _dot(act, wdb[...]).astype(BF16)
            _store_packed(ybuf, _bf16_words(y.astype(F32)), base)
            y_copy(i, slot).start()
            return c

        lax.fori_loop(0, n, step, 0)

        @pl.when(n >= 2)
        def _():
            y_copy(n - 2, n % 2).wait()

        y_copy(n - 1, (n - 1) % 2).wait()


def _moe(xs, w_gate, w_up, w_down, tile_start, tiles_e, tm):
    ne, d, ff = w_gate.shape
    tile_rows = tm * PACK_ROWS
    wmap = lambda e, ts, nt: (e, 0, 0)
    return pl.pallas_call(
        _moe_kernel,
        grid_spec=pltpu.PrefetchScalarGridSpec(
            num_scalar_prefetch=2,
            grid=(ne,),
            in_specs=[
                pl.BlockSpec(memory_space=pl.ANY),
                pl.BlockSpec((None, d, ff), wmap),
                pl.BlockSpec((None, d, ff), wmap),
                pl.BlockSpec((None, ff, d), wmap),
            ],
            out_specs=pl.BlockSpec(memory_space=pl.ANY),
            scratch_shapes=[pltpu.VMEM((2 * tile_rows, LANES), jnp.uint32), pltpu.VMEM((2 * tile_rows, LANES), jnp.uint32),
                            pltpu.VMEM((d, ff), BF16), pltpu.VMEM((d, ff), BF16), pltpu.VMEM((ff, d), BF16),
                            pltpu.SemaphoreType.DMA((2,)), pltpu.SemaphoreType.DMA((2,))],
        ),
        out_shape=jax.ShapeDtypeStruct(xs.shape, jnp.uint32),
        input_output_aliases={2: 0},
        compiler_params=_cparams("arbitrary"),
        name="moe_experts",
    )(tile_start, tiles_e, xs, w_gate, w_up, w_down)


COMBINE_ROWS = 64


def _combine_kernel(pos_hbm, y_hbm, base_ref, gate_ref, g_ref, b_ref, op_ref, os_ref,
                    pos_s, ybuf, gb_ref, f_ref, sem_pos, sem_row, *, n_prompt_tiles):
    i = pl.program_id(0)
    tm, d = base_ref.shape
    get_pos = pltpu.make_async_copy(pos_hbm.at[i], pos_s, sem_pos)
    get_pos.start()
    get_pos.wait()

    def row_copy(r, k):
        return pltpu.make_async_copy(_packed_rows(y_hbm, pos_s[k, r]), _packed_rows(ybuf, k * tm + r), sem_row)

    def issue(r, c):
        for k in range(TOP_K):
            row_copy(r, k).start()
        return c

    def drain(r, c):
        for k in range(TOP_K):
            row_copy(r, k).wait()
        return c

    lax.fori_loop(0, tm, issue, 0, unroll=DMA_UNROLL)
    gates = gate_ref[...]
    for k in range(TOP_K):
        gb_ref[k] = jnp.broadcast_to(gates[:, k:k + 1], (tm, LANES))
    lax.fori_loop(0, tm, drain, 0, unroll=DMA_UNROLL)

    for r0 in range(0, tm, COMBINE_ROWS):
        nr = min(COMBINE_ROWS, tm - r0)
        for s in range(PACK_ROWS):
            c_lo, c_hi = s * LANES, (PACK_ROWS + s) * LANES
            lo = base_ref[r0:r0 + nr, c_lo:c_lo + LANES]
            hi = base_ref[r0:r0 + nr, c_hi:c_hi + LANES]
            for k in range(TOP_K):
                w_lo, w_hi = _word_halves(ybuf[pl.ds((k * tm + r0) * PACK_ROWS + s, nr, stride=PACK_ROWS), :])
                g = gb_ref[k, r0:r0 + nr, :]
                lo = lo + g * w_lo
                hi = hi + g * w_hi
            f_ref[r0:r0 + nr, c_lo:c_lo + LANES] = lo
            f_ref[r0:r0 + nr, c_hi:c_hi + LANES] = hi
    y = _layer_norm(f_ref[...], g_ref[...], b_ref[...])

    @pl.when(i < n_prompt_tiles)
    def _():
        op_ref[...] = y

    @pl.when(i >= n_prompt_tiles)
    def _():
        os_ref[...] = y


def _combine(pos3, y_sorted, base, gates, ln_g, ln_b, tp, tm):
    t, d = base.shape
    n_p = tp // tm
    vec = pl.BlockSpec((1, d), lambda i: (0, 0))
    return pl.pallas_call(
        functools.partial(_combine_kernel, n_prompt_tiles=n_p),
        grid=(t // tm,),
        in_specs=[pl.BlockSpec(memory_space=pl.ANY), pl.BlockSpec(memory_space=pl.ANY),
                  pl.BlockSpec((tm, d), lambda i: (i, 0)), pl.BlockSpec((tm, TOP_K), lambda i: (i, 0)), vec, vec],
        out_specs=[pl.BlockSpec((tm, d), lambda i: (jnp.minimum(i, n_p - 1), 0)),
                   pl.BlockSpec((tm, d), lambda i: (jnp.maximum(i - n_p, 0), 0))],
        out_shape=[jax.ShapeDtypeStruct((tp, d), F32), jax.ShapeDtypeStruct((t - tp, d), F32)],
        scratch_shapes=[pltpu.SMEM((TOP_K, tm), jnp.int32),
                        pltpu.VMEM((TOP_K * tm * PACK_ROWS, LANES), jnp.uint32),
                        pltpu.VMEM((TOP_K, tm, LANES), F32),
                        pltpu.VMEM((tm, d), F32),
                        pltpu.SemaphoreType.DMA(()), pltpu.SemaphoreType.DMA(())],
        compiler_params=_cparams("arbitrary"),
        name="combine_ln2",
    )(pos3, y_sorted, base, gates, ln_g, ln_b)


def kernel(x_prompt, x_sample, cache_win_k, cache_win_v, state_conv, w_in, attn_sinks, conv_w, conv_b, conv_ln_g, conv_ln_b, attn_out_g, conv_out_g, w_out, ln1_g, ln1_b, router_w, router_b, exp_w_gate, exp_w_up, exp_w_down, sh_w_gate, sh_w_up, sh_w_down, ln2_g, ln2_b):
    assert w_in.shape[0] == DEPTH
    batch, seq, d = x_prompt.shape
    dec_batch, dec_seq, _ = x_sample.shape
    lc = cache_win_k.shape[2]
    ch = conv_w.shape[2]
    ne = router_w.shape[2]
    tp, ts = batch * seq, dec_batch * dec_seq
    t_all = tp + ts
    assert d == 2 * PACK_ROWS * LANES
    row2 = lambda a: a.reshape(1, a.shape[-1])

    w_in_b = w_in.reshape(w_in.shape[1:]).astype(BF16)
    w_out_b = w_out.reshape(w_out.shape[1:]).astype(BF16)
    rwt_b = router_w.reshape(router_w.shape[1:]).T.astype(BF16)
    sg_b = sh_w_gate.reshape(sh_w_gate.shape[1:]).astype(BF16)
    su_b = sh_w_up.reshape(sh_w_up.shape[1:]).astype(BF16)
    sd_b = sh_w_down.reshape(sh_w_down.shape[1:]).astype(BF16)
    sinks = attn_sinks.reshape(1, N_Q_HEADS)
    cw = conv_w.reshape(CONV_WIDTH, ch)
    cb, clg, clb = row2(conv_b), row2(conv_ln_g), row2(conv_ln_b)

    xp = x_prompt.reshape(tp, d)
    xs_in = x_sample.reshape(ts, d)
    cos_p, sin_p = _rope_tables(jnp.arange(seq))
    cos_s, sin_s = _rope_tables(jnp.tile(PAST_LEN + jnp.arange(dec_seq), dec_batch))

    assert NEW_PAD % dec_seq == 0 and dec_batch % SAMPLE_BB == 0 and seq % WINDOW == 0
    tm_p, tm_s = min(ROW_TILE, seq), min(ROW_TILE, ts)
    tm = min(tm_p, tm_s) // 2
    tn = min(ROUTE_TILE, 2 * tm)
    assert tp % tm == 0 and ts % tm == 0 and t_all % tn == 0
    qp, kp, vp, gp = _in_proj(xp, w_in_b, cos_p, sin_p, tm_p)
    attn_p = _prompt_attn(qp, kp, vp, sinks, batch, seq)
    conv_p = _conv_prompt(gp, cw, cb, clg, clb, batch, seq, WINDOW)
    qs, ks, vs, gs = _in_proj(xs_in, w_in_b, cos_s, sin_s, tm_s)
    ck = cache_win_k.reshape(dec_batch, lc, KV_WIDTH)
    cv = cache_win_v.reshape(dec_batch, lc, KV_WIDTH)
    attn_s = _sample_attn(qs, ks, vs, sinks, ck, cv, dec_batch, dec_seq)
    ext_s = jnp.concatenate([state_conv.reshape(dec_batch, CONV_WIDTH - 1, ch), gs.reshape(dec_batch, dec_seq, ch)], axis=1)
    conv_s = _conv_sample(ext_s, cw, cb, clg, clb, dec_seq).reshape(ts, ch)

    base, hbp, sct = _mix(attn_p, attn_s, conv_p, conv_s, xp, xs_in, w_out_b, row2(attn_out_g), row2(conv_out_g),
                          row2(ln1_g), row2(ln1_b), rwt_b, sg_b, su_b, sd_b, tm)
    bias_l = jnp.broadcast_to(router_b.reshape(ne, 1).astype(F32), (ne, LANES))
    eidx_t, gate_t, rank_t, cnt = _route(sct, bias_l, tn)

    et = EXPERT_TILE
    n_tiles = t_all * TOP_K // et + ne
    counts = cnt[:, 0]
    tiles_e = (counts + et - 1) // et
    tile_end = jnp.cumsum(tiles_e)
    tile_start = tile_end - tiles_e
    n_active = tile_end[-1]
    last_tile_row = jnp.where(tiles_e > 0, (tile_end - 1) * et, -1)
    fill_plan = jnp.concatenate([last_tile_row, n_active.reshape(1)]).astype(jnp.int32)

    pos3 = _positions((tile_start * et).astype(jnp.int32), eidx_t, rank_t, tm)
    xs = _dispatch(fill_plan, pos3, hbp, n_tiles * et, tm, et)
    wg = exp_w_gate.reshape(exp_w_gate.shape[1:])
    wu = exp_w_up.reshape(exp_w_up.shape[1:])
    wd = exp_w_down.reshape(exp_w_down.shape[1:])
    y_sorted = _moe(xs, wg, wu, wd, tile_start.astype(jnp.int32), tiles_e.astype(jnp.int32), et)
    y_p, y_s = _combine(pos3, y_sorted, base, gate_t.T, row2(ln2_g), row2(ln2_b), tp, tm)

    y_p = y_p.reshape(batch, seq, d)
    y_s = y_s.reshape(dec_batch, dec_seq, d)
    keep = min(WINDOW, seq)
    kv5 = lambda a, b: a.reshape(1, b, -1, N_KV_HEADS, HEAD_DIM)
    new_k_p = kv5(kp.reshape(batch, seq, KV_WIDTH)[:, seq - keep:], batch)
    new_v_p = kv5(vp.reshape(batch, seq, KV_WIDTH)[:, seq - keep:], batch)
    new_c_p = gp.reshape(batch, seq, ch)[:, seq - (CONV_WIDTH - 1):].reshape(1, batch, CONV_WIDTH - 1, ch)
    keep_s = min(WINDOW, PAST_LEN + dec_seq)
    k_all = jnp.concatenate([ck, ks.reshape(dec_batch, dec_seq, KV_WIDTH)], axis=1)
    v_all = jnp.concatenate([cv, vs.reshape(dec_batch, dec_seq, KV_WIDTH)], axis=1)
    new_k_s = kv5(k_all[:, k_all.shape[1] - keep_s:], dec_batch)
    new_v_s = kv5(v_all[:, v_all.shape[1] - keep_s:], dec_batch)
    new_c_s = ext_s[:, dec_seq:].reshape(1, dec_batch, CONV_WIDTH - 1, ch)
    return (y_p, y_s, new_k_p, new_v_p, new_c_p, new_k_s, new_v_s, new_c_s)
```

```python
import functools

import jax
import jax.numpy as jnp
from jax import lax
from jax.experimental import pallas as pl
from jax.experimental.pallas import tpu as pltpu

F32 = jnp.float32
BF16 = jnp.bfloat16

DEPTH = 1
PAST_LEN = 16384
HEAD_DIM = 64
N_Q_HEADS = 16
N_KV_HEADS = 4
Q_PER_KV = N_Q_HEADS // N_KV_HEADS
ATTN_WIDTH = N_Q_HEADS * HEAD_DIM
KV_WIDTH = N_KV_HEADS * HEAD_DIM
WINDOW = 128
ROPE_THETA = 10000.0
CONV_WIDTH = 31
CONV_HALO = 32
TOP_K = 8
N_EXPERT_GROUPS = 8
TOPK_GROUPS = 4
ROUTED_SCALE = 2.5
LN_EPS = 1e-5
RMS_EPS = 1e-6
DEEPNORM_ALPHA = (2.0 * DEPTH) ** 0.25

LANES = 128
VMEM_LIMIT = 56 * 1024 * 1024
ROW_TILE = 512
EXPERT_TILE = 256
PACK_ROWS = 8
ROUTE_TILE = 512
DMA_UNROLL = 8


def _cparams(*sem):
    return pltpu.CompilerParams(dimension_semantics=sem, vmem_limit_bytes=VMEM_LIMIT)


def _dot(a, b):
    return jnp.dot(a, b, preferred_element_type=F32)


def _layer_norm(x, g, b):
    xc = x - jnp.mean(x, axis=-1, keepdims=True)
    var = jnp.mean(xc * xc, axis=-1, keepdims=True)
    return xc * lax.rsqrt(var + LN_EPS) * g + b


def _rms_norm(x, g):
    ms = jnp.mean(x * x, axis=-1, keepdims=True)
    return x * lax.rsqrt(ms + RMS_EPS) * g


def _sigmoid(x):
    return 1.0 / (1.0 + jnp.exp(-x))


def _rope_tables(pos):
    half = HEAD_DIM // 2
    inv_freq = 1.0 / (ROPE_THETA ** (jnp.arange(half, dtype=F32) * (2.0 / HEAD_DIM)))
    ang = pos.astype(F32)[:, None] * inv_freq[None, :]
    cos, sin = jnp.cos(ang), jnp.sin(ang)
    reps = LANES // HEAD_DIM
    return jnp.tile(cos, (1, 2 * reps)), jnp.tile(jnp.concatenate([-sin, sin], axis=1), (1, reps))


def _in_proj_kernel(x_ref, w_ref, cos_ref, sin_ref, q_ref, k_ref, v_ref, g_ref):
    xb = x_ref[...].astype(BF16)
    cos = cos_ref[...]
    sin = sin_ref[...]
    lane = lax.broadcasted_iota(jnp.int32, cos.shape, 1)
    first_half = (lane % HEAD_DIM) < (HEAD_DIM // 2)

    def rope(acc):
        outs = []
        for c in range(acc.shape[1] // LANES):
            a = acc[:, c * LANES:(c + 1) * LANES]
            partner = jnp.where(first_half, pltpu.roll(a, LANES - HEAD_DIM // 2, 1),
                                pltpu.roll(a, HEAD_DIM // 2, 1))
            outs.append(a * cos + partner * sin)
        return jnp.concatenate(outs, axis=1)

    o_k = ATTN_WIDTH
    o_v = o_k + KV_WIDTH
    o_a = o_v + KV_WIDTH
    ch = (w_ref.shape[1] - o_a) // 2
    q = rope(_dot(xb, w_ref[:, 0:o_k]))
    q_ref[...] = (q * (HEAD_DIM ** -0.5)).astype(BF16)
    k_ref[...] = rope(_dot(xb, w_ref[:, o_k:o_v]))
    v_ref[...] = _dot(xb, w_ref[:, o_v:o_a])
    a = _dot(xb, w_ref[:, o_a:o_a + ch])
    gate = _dot(xb, w_ref[:, o_a + ch:o_a + 2 * ch])
    g_ref[...] = a * _sigmoid(gate)


def _in_proj(x2d, w_bf16, cos, sin, tm):
    t, d = x2d.shape
    n = w_bf16.shape[1]
    ch = (n - ATTN_WIDTH - 2 * KV_WIDTH) // 2
    ntab = cos.shape[0] // tm
    return pl.pallas_call(
        _in_proj_kernel,
        grid=(t // tm,),
        in_specs=[
            pl.BlockSpec((tm, d), lambda i: (i, 0)),
            pl.BlockSpec((d, n), lambda i: (0, 0), pipeline_mode=pl.Buffered(1)),
            pl.BlockSpec((tm, LANES), lambda i: (i % ntab, 0)),
            pl.BlockSpec((tm, LANES), lambda i: (i % ntab, 0)),
        ],
        out_specs=[
            pl.BlockSpec((tm, ATTN_WIDTH), lambda i: (i, 0)),
            pl.BlockSpec((tm, KV_WIDTH), lambda i: (i, 0)),
            pl.BlockSpec((tm, KV_WIDTH), lambda i: (i, 0)),
            pl.BlockSpec((tm, ch), lambda i: (i, 0)),
        ],
        out_shape=[
            jax.ShapeDtypeStruct((t, ATTN_WIDTH), BF16),
            jax.ShapeDtypeStruct((t, KV_WIDTH), F32),
            jax.ShapeDtypeStruct((t, KV_WIDTH), F32),
            jax.ShapeDtypeStruct((t, ch), F32),
        ],
        compiler_params=_cparams("parallel"),
        name="in_proj",
    )(x2d, w_bf16, cos, sin)


def _sink_softmax_pv(s, mask, sink, v_bf16):
    s = jnp.where(mask, s, -jnp.inf)
    m = jnp.maximum(jnp.max(s, axis=-1, keepdims=True), sink)
    e = jnp.exp(s - m)
    den = jnp.sum(e, axis=-1, keepdims=True) + jnp.exp(sink - m)
    return _dot(e.astype(BF16), v_bf16) / den


def _prompt_attn_kernel(sink_ref, q_ref, kc_ref, kp_ref, vc_ref, vp_ref, o_ref):
    j = pl.program_id(1)
    q = q_ref[...]
    kb = jnp.concatenate([kp_ref[...], kc_ref[...]], axis=0).astype(BF16)
    vb = jnp.concatenate([vp_ref[...], vc_ref[...]], axis=0).astype(BF16)
    r = lax.broadcasted_iota(jnp.int32, (WINDOW, 2 * WINDOW), 0)
    s_idx = lax.broadcasted_iota(jnp.int32, (WINDOW, 2 * WINDOW), 1)
    mask = (s_idx > r) & (s_idx <= r + WINDOW) & ((s_idx >= WINDOW) | (j > 0))
    outs = []
    for kv in range(N_KV_HEADS):
        kk = kb[:, kv * HEAD_DIM:(kv + 1) * HEAD_DIM]
        vv = vb[:, kv * HEAD_DIM:(kv + 1) * HEAD_DIM]
        for g in range(Q_PER_KV):
            h = kv * Q_PER_KV + g
            qh = q[:, h * HEAD_DIM:(h + 1) * HEAD_DIM]
            s = lax.dot_general(qh, kk, (((1,), (1,)), ((), ())), preferred_element_type=F32)
            outs.append(_sink_softmax_pv(s, mask, sink_ref[0, h], vv))
    o_ref[...] = jnp.concatenate(outs, axis=1)


def _prompt_attn(q, k, v, sinks, batch, seq):
    nb = seq // WINDOW
    cur = lambda b, j: (b * nb + j, 0)
    prev = lambda b, j: (b * nb + jnp.maximum(j - 1, 0), 0)
    return pl.pallas_call(
        _prompt_attn_kernel,
        grid=(batch, nb),
        in_specs=[
            pl.BlockSpec(memory_space=pltpu.SMEM),
            pl.BlockSpec((WINDOW, ATTN_WIDTH), cur),
            pl.BlockSpec((WINDOW, KV_WIDTH), cur),
            pl.BlockSpec((WINDOW, KV_WIDTH), prev),
            pl.BlockSpec((WINDOW, KV_WIDTH), cur),
            pl.BlockSpec((WINDOW, KV_WIDTH), prev),
        ],
        out_specs=pl.BlockSpec((WINDOW, ATTN_WIDTH), cur),
        out_shape=jax.ShapeDtypeStruct((batch * seq, ATTN_WIDTH), F32),
        compiler_params=_cparams("parallel", "arbitrary"),
        name="prompt_attn",
    )(sinks, q, k, k, v, v)


SAMPLE_BB = 8
NEW_PAD = 8


def _sample_attn_kernel(sink_ref, q_ref, kn_ref, vn_ref, kc_ref, vc_ref, o_ref, *, dec_seq):
    lc = kc_ref.shape[1]
    rows = dec_seq * Q_PER_KV
    per_pad = NEW_PAD // dec_seq
    t_q = lax.broadcasted_iota(jnp.int32, (rows, lc + NEW_PAD), 0) // Q_PER_KV
    col = lax.broadcasted_iota(jnp.int32, (rows, lc + NEW_PAD), 1)
    g_idx = lax.broadcasted_iota(jnp.int32, (rows, 1), 0) % Q_PER_KV
    for bi in range(SAMPLE_BB):
        lo = (bi % per_pad) * dec_seq
        base = (bi // per_pad) * NEW_PAD
        kall = jnp.concatenate([kc_ref[bi], kn_ref[base:base + NEW_PAD, :]], axis=0).astype(BF16)
        vall = jnp.concatenate([vc_ref[bi], vn_ref[base:base + NEW_PAD, :]], axis=0).astype(BF16)
        new_t = col - lc - lo
        mask = ((col < lc) & (col + WINDOW > t_q + lc)) | ((new_t >= 0) & (new_t <= t_q))
        for kv in range(N_KV_HEADS):
            kk = kall[:, kv * HEAD_DIM:(kv + 1) * HEAD_DIM]
            vv = vall[:, kv * HEAD_DIM:(kv + 1) * HEAD_DIM]
            sink = jnp.zeros((rows, 1), F32)
            for g in range(Q_PER_KV):
                sink = jnp.where(g_idx == g, sink_ref[0, kv * Q_PER_KV + g], sink)
            s = lax.dot_general(q_ref[bi, kv], kk, (((1,), (1,)), ((), ())), preferred_element_type=F32)
            o_ref[bi, kv] = _sink_softmax_pv(s, mask, sink, vv)


def _sample_attn(q, k, v, sinks, cache_k, cache_v, dec_batch, dec_seq):
    lc = cache_k.shape[1]
    rows = dec_seq * Q_PER_KV
    q5 = q.reshape(dec_batch, dec_seq, N_KV_HEADS, Q_PER_KV, HEAD_DIM).transpose(0, 2, 1, 3, 4)
    q4 = q5.reshape(dec_batch, N_KV_HEADS, rows, HEAD_DIM)
    blk_q = pl.BlockSpec((SAMPLE_BB, N_KV_HEADS, rows, HEAD_DIM), lambda i: (i, 0, 0, 0))
    blk_new = pl.BlockSpec((SAMPLE_BB * dec_seq, KV_WIDTH), lambda i: (i, 0))
    blk_cache = pl.BlockSpec((SAMPLE_BB, lc, KV_WIDTH), lambda i: (i, 0, 0))
    o = pl.pallas_call(
        functools.partial(_sample_attn_kernel, dec_seq=dec_seq),
        grid=(dec_batch // SAMPLE_BB,),
        in_specs=[pl.BlockSpec(memory_space=pltpu.SMEM), blk_q, blk_new, blk_new, blk_cache, blk_cache],
        out_specs=blk_q,
        out_shape=jax.ShapeDtypeStruct(q4.shape, F32),
        compiler_params=_cparams("parallel"),
        name="sample_attn",
    )(sinks, q4, k, v, cache_k, cache_v)
    o = o.reshape(dec_batch, N_KV_HEADS, dec_seq, Q_PER_KV, HEAD_DIM).transpose(0, 2, 1, 3, 4)
    return o.reshape(dec_batch * dec_seq, ATTN_WIDTH)


CONV_ROWS = 32
CONV_ROWS_GROUPED = 64


def _conv_taps(ext_ref, w_ref, y_ref, n_out, first_row):
    ch = y_ref.shape[-1]
    for r0 in range(0, n_out, CONV_ROWS):
        nr = min(CONV_ROWS, n_out - r0)
        for c0 in range(0, ch, LANES):
            acc = jnp.zeros((nr, LANES), F32)
            for j in range(CONV_WIDTH):
                acc = acc + ext_ref[pl.ds(first_row + r0 + j, nr), c0:c0 + LANES] * w_ref[j:j + 1, c0:c0 + LANES]
            y_ref[r0:r0 + nr, c0:c0 + LANES] = acc


def _conv_taps_grouped(ext_ref, w_ref, y_ref, n_out, first_row):
    ch = y_ref.shape[-1]
    sub = 8
    groups = [[j for j in range(CONV_WIDTH) if (first_row + j) % sub == res] for res in range(sub)]
    for r0 in range(0, n_out, CONV_ROWS_GROUPED):
        nr = min(CONV_ROWS_GROUPED, n_out - r0)
        for c0 in range(0, ch, LANES):
            y = jnp.zeros((nr, LANES), F32)
            for res, taps in enumerate(groups):
                rows = nr if res == 0 else nr + sub
                acc = jnp.zeros((rows, LANES), F32)
                for j in taps:
                    acc = acc + ext_ref[pl.ds(r0 + first_row + j - res, rows), c0:c0 + LANES] * w_ref[j:j + 1, c0:c0 + LANES]
                if taps:
                    y = y + acc[res:res + nr, :]
            y_ref[r0:r0 + nr, c0:c0 + LANES] = y


def _conv_post(y, b_ref, g_ref, beta_ref):
    z = _layer_norm(y + b_ref[...], g_ref[...], beta_ref[...])
    return z * _sigmoid(z)


def _conv_prompt_kernel(cur_ref, halo_ref, w_ref, b_ref, g_ref, beta_ref, o_ref, ext_ref, y_ref):
    i = pl.program_id(1)
    tt = cur_ref.shape[0]
    ext_ref[0:CONV_HALO, :] = jnp.where(i > 0, halo_ref[...], 0.0)
    ext_ref[CONV_HALO:CONV_HALO + tt, :] = cur_ref[...]
    _conv_taps_grouped(ext_ref, w_ref, y_ref, tt, CONV_HALO - (CONV_WIDTH - 1))
    o_ref[...] = _conv_post(y_ref[...], b_ref, g_ref, beta_ref)


def _conv_prompt(g, w, b, ln_g, ln_b, batch, seq, tt):
    ch = g.shape[1]
    nt = seq // tt
    hb = tt // CONV_HALO
    vec = pl.BlockSpec((1, ch), lambda bi, i: (0, 0))
    return pl.pallas_call(
        _conv_prompt_kernel,
        grid=(batch, nt),
        in_specs=[
            pl.BlockSpec((tt, ch), lambda bi, i: (bi * nt + i, 0)),
            pl.BlockSpec((CONV_HALO, ch), lambda bi, i: (jnp.maximum((bi * nt + i) * hb - 1, 0), 0)),
            pl.BlockSpec((CONV_WIDTH, ch), lambda bi, i: (0, 0)),
            vec, vec, vec,
        ],
        out_specs=pl.BlockSpec((tt, ch), lambda bi, i: (bi * nt + i, 0)),
        out_shape=jax.ShapeDtypeStruct(g.shape, F32),
        scratch_shapes=[pltpu.VMEM((CONV_HALO + tt, ch), F32), pltpu.VMEM((tt, ch), F32)],
        compiler_params=_cparams("parallel", "arbitrary"),
        name="conv_prompt",
    )(g, g, w, b, ln_g, ln_b)


def _conv_sample_kernel(ext_ref, w_ref, b_ref, g_ref, beta_ref, o_ref, y_ref):
    nb, _, ch = ext_ref.shape
    dec = o_ref.shape[1]
    for bi in range(nb):
        _conv_taps(ext_ref.at[bi], w_ref, y_ref.at[bi], dec, 0)
    for bi in range(nb):
        o_ref[bi] = _conv_post(y_ref[bi], b_ref, g_ref, beta_ref)


def _conv_sample(ext, w, b, ln_g, ln_b, dec_seq):
    nb, rows, ch = ext.shape
    vec = pl.BlockSpec((1, ch), lambda i: (0, 0))
    return pl.pallas_call(
        _conv_sample_kernel,
        grid=(nb // SAMPLE_BB,),
        in_specs=[
            pl.BlockSpec((SAMPLE_BB, rows, ch), lambda i: (i, 0, 0)),
            pl.BlockSpec((CONV_WIDTH, ch), lambda i: (0, 0)),
            vec, vec, vec,
        ],
        out_specs=pl.BlockSpec((SAMPLE_BB, dec_seq, ch), lambda i: (i, 0, 0)),
        out_shape=jax.ShapeDtypeStruct((nb, dec_seq, ch), F32),
        scratch_shapes=[pltpu.VMEM((SAMPLE_BB, dec_seq, ch), F32)],
        compiler_params=_cparams("parallel"),
        name="conv_sample",
    )(ext, w, b, ln_g, ln_b)


def _bf16_words(vals):
    half = vals.shape[1] // 2
    bits = lax.bitcast_convert_type(vals, jnp.uint32)
    return (bits[:, :half] >> 16) | (bits[:, half:] & jnp.uint32(0xFFFF0000))


def _word_halves(words):
    lo = lax.bitcast_convert_type(words << 16, F32)
    hi = lax.bitcast_convert_type(words & jnp.uint32(0xFFFF0000), F32)
    return lo, hi


def _store_packed(ref, words, base=0):
    n = words.shape[0]
    for s in range(PACK_ROWS):
        ref[pl.ds(base + s, n, stride=PACK_ROWS), :] = words[:, s * LANES:(s + 1) * LANES]


def _mix_kernel(ap_ref, as_ref, cp_ref, cs_ref, xp_ref, xs_ref, wo_ref, ag_ref, cg_ref, g1_ref, b1_ref, rwt_ref,
                sg_ref, su_ref, sd_ref, base_ref, hbp_ref, sct_ref, *, n_prompt_tiles):
    is_p = pl.program_id(0) < n_prompt_tiles
    attn = jnp.where(is_p, ap_ref[...], as_ref[...])
    conv = jnp.where(is_p, cp_ref[...], cs_ref[...])
    x = jnp.where(is_p, xp_ref[...], xs_ref[...])
    aw = attn.shape[1]
    a = _rms_norm(attn, ag_ref[...]).astype(BF16)
    c = _rms_norm(conv, cg_ref[...]).astype(BF16)
    mixed = _dot(a, wo_ref[0:aw, :]) + _dot(c, wo_ref[aw:, :])
    h = _layer_norm(DEEPNORM_ALPHA * x + mixed, g1_ref[...], b1_ref[...])
    hb = h.astype(BF16)
    _store_packed(hbp_ref, _bf16_words(hb.astype(F32)))
    logits_t = lax.dot_general(rwt_ref[...], hb, (((1,), (1,)), ((), ())), preferred_element_type=F32)
    sct_ref[...] = _sigmoid(logits_t)
    h1 = _dot(hb, sg_ref[...])
    act = (h1 * _sigmoid(h1) * _dot(hb, su_ref[...])).astype(BF16)
    base_ref[...] = DEEPNORM_ALPHA * h + _dot(act, sd_ref[...])


def _mix(attn_p, attn_s, conv_p, conv_s, xp, xs, wo, ag, cg, g1, b1, rwt, sg, su, sd, tm):
    tp, d = xp.shape
    ts = xs.shape[0]
    n_p, n_s = tp // tm, ts // tm
    ne = rwt.shape[0]
    t = tp + ts
    prow = lambda w: pl.BlockSpec((tm, w), lambda i: (jnp.minimum(i, n_p - 1), 0))
    srow = lambda w: pl.BlockSpec((tm, w), lambda i: (jnp.maximum(i - n_p, 0), 0))
    full = lambda a: pl.BlockSpec(a.shape, lambda i: (0,) * a.ndim, pipeline_mode=pl.Buffered(1))
    aw, cw = attn_p.shape[1], conv_p.shape[1]
    return pl.pallas_call(
        functools.partial(_mix_kernel, n_prompt_tiles=n_p),
        grid=(n_p + n_s,),
        in_specs=[prow(aw), srow(aw), prow(cw), srow(cw), prow(d), srow(d)]
        + [full(a) for a in (wo, ag, cg, g1, b1, rwt, sg, su, sd)],
        out_specs=[pl.BlockSpec((tm, d), lambda i: (i, 0)),
                   pl.BlockSpec((tm * PACK_ROWS, LANES), lambda i: (i, 0)),
                   pl.BlockSpec((ne, tm), lambda i: (0, i))],
        out_shape=[jax.ShapeDtypeStruct((t, d), F32),
                   jax.ShapeDtypeStruct((t * PACK_ROWS, LANES), jnp.uint32),
                   jax.ShapeDtypeStruct((ne, t), F32)],
        compiler_params=_cparams("arbitrary"),
        name="mix_ln1",
    )(attn_p, attn_s, conv_p, conv_s, xp, xs, wo, ag, cg, g1, b1, rwt, sg, su, sd)


def _route_kernel(sc_ref, bias_ref, tri_ref, eidx_ref, gate_ref, rank_ref, cnt_ref, carry_ref, mask_ref, eidf_ref):
    ne, tn = sc_ref.shape
    per = ne // N_EXPERT_GROUPS
    ninf = -jnp.inf

    @pl.when(pl.program_id(0) == 0)
    def _():
        carry_ref[...] = jnp.zeros_like(carry_ref)

    erow = lax.broadcasted_iota(jnp.int32, (ne, LANES), 0).astype(F32)
    grow = lax.broadcasted_iota(jnp.int32, (per, LANES), 0).astype(F32)
    gi = lax.broadcasted_iota(jnp.int32, (N_EXPERT_GROUPS, LANES), 0).astype(F32)
    colmax = lambda v: jnp.max(v, axis=0, keepdims=True)
    first_at = lambda v, m, idx, n: jnp.min(jnp.where(v == m, idx, float(n)), axis=0, keepdims=True)

    for c in range(tn // LANES):
        lanes = slice(c * LANES, (c + 1) * LANES)
        sc = sc_ref[:, lanes]
        sel = sc + bias_ref[...]
        gs = []
        for g in range(N_EXPERT_GROUPS):
            v = sel[g * per:(g + 1) * per, :]
            m1 = colmax(v)
            m2 = colmax(jnp.where(grow == first_at(v, m1, grow, per), ninf, v))
            gs.append(m1 + m2)
        grp = jnp.concatenate(gs, axis=0)
        gsel = jnp.zeros_like(grp)
        for _ in range(TOPK_GROUPS):
            m = colmax(grp)
            hit = gi == first_at(grp, m, gi, N_EXPERT_GROUPS)
            gsel = jnp.where(hit, 1.0, gsel)
            grp = jnp.where(hit, ninf, grp)
        emask = jnp.concatenate([jnp.broadcast_to(gsel[g:g + 1, :], (per, LANES)) for g in range(N_EXPERT_GROUPS)], axis=0)
        masked = jnp.where(emask > 0.5, sel, ninf)
        selm = jnp.zeros_like(sel)
        idxs, gts = [], []
        for _ in range(TOP_K):
            ix = first_at(masked, colmax(masked), erow, ne)
            hit = erow == ix
            gts.append(jnp.sum(jnp.where(hit, sc, 0.0), axis=0, keepdims=True))
            idxs.append(ix)
            masked = jnp.where(hit, ninf, masked)
            selm = jnp.where(hit, 1.0, selm)
        gsum = gts[0]
        for g in gts[1:]:
            gsum = gsum + g
        eidf = jnp.concatenate(idxs, axis=0)
        eidf_ref[:, lanes] = eidf
        eidx_ref[:, lanes] = eidf.astype(jnp.int32)
        gate_ref[:, lanes] = jnp.concatenate([g / gsum * ROUTED_SCALE for g in gts], axis=0)
        mask_ref[:, lanes] = selm

    selm_all = mask_ref[...]
    incl = _dot(selm_all.astype(BF16), tri_ref[...])
    carry = carry_ref[...]
    for c in range(tn // LANES):
        lanes = slice(c * LANES, (c + 1) * LANES)
        before = incl[:, lanes] - selm_all[:, lanes] + carry
        eidf = eidf_ref[:, lanes]
        rk = [jnp.sum(jnp.where(erow == eidf[k:k + 1, :], before, 0.0), axis=0, keepdims=True) for k in range(TOP_K)]
        rank_ref[:, lanes] = jnp.concatenate(rk, axis=0).astype(jnp.int32)
    total = carry + jnp.broadcast_to(incl[:, tn - 1:tn], (ne, LANES))
    carry_ref[...] = total
    cnt_ref[...] = total.astype(jnp.int32)


def _route(sct, bias_l, tn):
    ne, t = sct.shape
    tri = jnp.triu(jnp.ones((tn, tn), BF16))
    col = pl.BlockSpec((TOP_K, tn), lambda i: (0, i))
    return pl.pallas_call(
        _route_kernel,
        grid=(t // tn,),
        in_specs=[pl.BlockSpec((ne, tn), lambda i: (0, i)),
                  pl.BlockSpec((ne, LANES), lambda i: (0, 0)),
                  pl.BlockSpec((tn, tn), lambda i: (0, 0))],
        out_specs=[col, col, col, pl.BlockSpec((ne, LANES), lambda i: (0, 0))],
        out_shape=[jax.ShapeDtypeStruct((TOP_K, t), jnp.int32), jax.ShapeDtypeStruct((TOP_K, t), F32),
                   jax.ShapeDtypeStruct((TOP_K, t), jnp.int32), jax.ShapeDtypeStruct((ne, LANES), jnp.int32)],
        scratch_shapes=[pltpu.VMEM((ne, LANES), F32), pltpu.VMEM((ne, tn), F32), pltpu.VMEM((TOP_K, tn), F32)],
        compiler_params=_cparams("arbitrary"),
        name="route",
    )(sct, bias_l, tri)


def _pos_kernel(pstart_ref, eidx_ref, rank_ref, pos_ref):
    eid = eidx_ref[...]
    base = lax.fori_loop(0, pstart_ref.shape[0], lambda e, acc: jnp.where(eid == e, pstart_ref[e], acc),
                         jnp.zeros_like(eid), unroll=DMA_UNROLL)
    pos_ref[...] = base + rank_ref[...]


def _positions(pstart, eidx_t, rank_t, tm):
    t = eidx_t.shape[1]
    col = pl.BlockSpec((TOP_K, tm), lambda i, ps: (0, i))
    return pl.pallas_call(
        _pos_kernel,
        grid_spec=pltpu.PrefetchScalarGridSpec(
            num_scalar_prefetch=1, grid=(t // tm,), in_specs=[col, col],
            out_specs=pl.BlockSpec((None, TOP_K, tm), lambda i, ps: (i, 0, 0))),
        out_shape=jax.ShapeDtypeStruct((t // tm, TOP_K, tm), jnp.int32),
        compiler_params=_cparams("parallel"),
        name="positions",
    )(pstart, eidx_t, rank_t)


def _packed_rows(ref, row):
    return ref.at[pl.ds(pl.multiple_of(row * PACK_ROWS, PACK_ROWS), PACK_ROWS), :]


def _dispatch_kernel(last_ref, pos_hbm, hb_ref, xs_hbm, pos_s, zero_ref, sem_pos, sem_zero, sem_row):
    i = pl.program_id(0)
    tm = pos_s.shape[1]
    tile_rows = zero_ref.shape[0]

    @pl.when(i == 0)
    def _():
        zero_ref[...] = jnp.zeros_like(zero_ref)

        def fill(e):
            start = pl.multiple_of(last_ref[e] * PACK_ROWS, PACK_ROWS)
            return pltpu.make_async_copy(zero_ref, xs_hbm.at[pl.ds(start, tile_rows), :], sem_zero)

        def go(e, c):
            @pl.when(last_ref[e] >= 0)
            def _():
                fill(e).start()
            return c

        def done(e, c):
            @pl.when(last_ref[e] >= 0)
            def _():
                fill(e).wait()
            return c

        def tail(t):
            start = pl.multiple_of(t * tile_rows, PACK_ROWS)
            return pltpu.make_async_copy(zero_ref, xs_hbm.at[pl.ds(start, tile_rows), :], sem_zero)

        def tail_go(t, c):
            tail(t).start()
            return c

        def tail_done(t, c):
            tail(t).wait()
            return c

        n_experts = last_ref.shape[0] - 1
        n_active, n_tiles = last_ref[n_experts], xs_hbm.shape[0] // tile_rows
        lax.fori_loop(0, n_experts, go, 0)
        lax.fori_loop(n_active, n_tiles, tail_go, 0)
        lax.fori_loop(0, n_experts, done, 0)
        lax.fori_loop(n_active, n_tiles, tail_done, 0)

    get_pos = pltpu.make_async_copy(pos_hbm.at[i], pos_s, sem_pos)
    get_pos.start()
    get_pos.wait()

    def row_copy(r, k):
        return pltpu.make_async_copy(_packed_rows(hb_ref, r), _packed_rows(xs_hbm, pos_s[k, r]), sem_row)

    def issue(r, c):
        for k in range(TOP_K):
            row_copy(r, k).start(priority=k % 2)
        return c

    def drain(r, c):
        for k in range(TOP_K):
            row_copy(r, k).wait()
        return c

    lax.fori_loop(0, tm, issue, 0, unroll=DMA_UNROLL)
    lax.fori_loop(0, tm, drain, 0, unroll=DMA_UNROLL)


def _dispatch(fill_plan, pos3, hbp, n_rows, tm, expert_tile):
    n_tok_tiles = pos3.shape[0]
    return pl.pallas_call(
        _dispatch_kernel,
        grid_spec=pltpu.PrefetchScalarGridSpec(
            num_scalar_prefetch=1, grid=(n_tok_tiles,),
            in_specs=[pl.BlockSpec(memory_space=pl.ANY),
                      pl.BlockSpec((tm * PACK_ROWS, LANES), lambda i, lt: (i, 0))],
            out_specs=pl.BlockSpec(memory_space=pl.ANY),
            scratch_shapes=[pltpu.SMEM((TOP_K, tm), jnp.int32),
                            pltpu.VMEM((expert_tile * PACK_ROWS, LANES), jnp.uint32),
                            pltpu.SemaphoreType.DMA(()), pltpu.SemaphoreType.DMA(()), pltpu.SemaphoreType.DMA(())]),
        out_shape=jax.ShapeDtypeStruct((n_rows * PACK_ROWS, LANES), jnp.uint32),
        compiler_params=_cparams("arbitrary"),
        name="dispatch",
    )(fill_plan, pos3, hbp)


def _moe_kernel(first_ref, count_ref, x_hbm, wg_ref, wu_ref, wd_ref, y_hbm, xbuf, ybuf, xrem, yrem, wgb, wub, wdb,
                sem_x, sem_y, sem_r):
    e = pl.program_id(0)
    n, t0 = count_ref[e], first_ref[e]
    tile_rows = xrem.shape[0]
    pair_rows = 2 * tile_rows
    n_pairs = n // 2
    odd = n - 2 * n_pairs

    def hbm_rows(ref, tile, rows):
        return ref.at[pl.ds(pl.multiple_of((t0 + tile) * tile_rows, PACK_ROWS), rows), :]

    def slot_rows(ref, slot):
        return ref.at[pl.ds(pl.multiple_of(slot * pair_rows, PACK_ROWS), pair_rows), :]

    def x_copy(j, slot):
        return pltpu.make_async_copy(hbm_rows(x_hbm, 2 * j, pair_rows), slot_rows(xbuf, slot), sem_x.at[slot])

    def y_copy(j, slot):
        return pltpu.make_async_copy(slot_rows(ybuf, slot), hbm_rows(y_hbm, 2 * j, pair_rows), sem_y.at[slot])

    x_last = pltpu.make_async_copy(hbm_rows(x_hbm, 2 * n_pairs, tile_rows), xrem, sem_r.at[0])
    y_last = pltpu.make_async_copy(yrem, hbm_rows(y_hbm, 2 * n_pairs, tile_rows), sem_r.at[1])

    def experts_mlp(src, src_base, dst, dst_base, rows):
        tm = rows // PACK_ROWS
        halves = [_word_halves(src[pl.ds(src_base + s, tm, stride=PACK_ROWS), :]) for s in range(PACK_ROWS)]
        x = jnp.concatenate([h[0] for h in halves] + [h[1] for h in halves], axis=1).astype(BF16)
        h1 = _dot(x, wgb[...])
        act = (h1 * _sigmoid(h1) * _dot(x, wub[...])).astype(BF16)
        y = _dot(act, wdb[...]).astype(BF16)
        _store_packed(dst, _bf16_words(y.astype(F32)), dst_base)

    @pl.when(n > 0)
    def _():
        @pl.when(n_pairs > 0)
        def _():
            x_copy(0, 0).start()

        @pl.when(odd == 1)
        def _():
            x_last.start()

        wgb[...] = wg_ref[...].astype(BF16)
        wub[...] = wu_ref[...].astype(BF16)
        wdb[...] = wd_ref[...].astype(BF16)

        @pl.when(odd == 1)
        def _():
            x_last.wait()
            experts_mlp(xrem, 0, yrem, 0, tile_rows)
            y_last.start()

        def step(j, c):
            slot = j % 2
            x_copy(j, slot).wait()

            @pl.when(j + 1 < n_pairs)
            def _():
                x_copy(j + 1, 1 - slot).start()

            @pl.when(j >= 2)
            def _():
                y_copy(j - 2, slot).wait()

            experts_mlp(xbuf, slot * pair_rows, ybuf, slot * pair_rows, pair_rows)
            y_copy(j, slot).start()
            return c

        lax.fori_loop(0, n_pairs, step, 0)

        @pl.when(n_pairs >= 2)
        def _():
            y_copy(n_pairs - 2, n_pairs % 2).wait()

        @pl.when(n_pairs >= 1)
        def _():
            y_copy(n_pairs - 1, (n_pairs - 1) % 2).wait()

        @pl.when(odd == 1)
        def _():
            y_last.wait()


def _moe(xs, w_gate, w_up, w_down, tile_start, tiles_e, tm):
    ne, d, ff = w_gate.shape
    tile_rows = tm * PACK_ROWS
    wmap = lambda e, ts, nt: (e, 0, 0)
    return pl.pallas_call(
        _moe_kernel,
        grid_spec=pltpu.PrefetchScalarGridSpec(
            num_scalar_prefetch=2,
            grid=(ne,),
            in_specs=[
                pl.BlockSpec(memory_space=pl.ANY),
                pl.BlockSpec((None, d, ff), wmap),
                pl.BlockSpec((None, d, ff), wmap),
                pl.BlockSpec((None, ff, d), wmap),
            ],
            out_specs=pl.BlockSpec(memory_space=pl.ANY),
            scratch_shapes=[pltpu.VMEM((4 * tile_rows, LANES), jnp.uint32), pltpu.VMEM((4 * tile_rows, LANES), jnp.uint32),
                            pltpu.VMEM((tile_rows, LANES), jnp.uint32), pltpu.VMEM((tile_rows, LANES), jnp.uint32),
                            pltpu.VMEM((d, ff), BF16), pltpu.VMEM((d, ff), BF16), pltpu.VMEM((ff, d), BF16),
                            pltpu.SemaphoreType.DMA((2,)), pltpu.SemaphoreType.DMA((2,)), pltpu.SemaphoreType.DMA((2,))],
        ),
        out_shape=jax.ShapeDtypeStruct(xs.shape, jnp.uint32),
        input_output_aliases={2: 0},
        compiler_params=_cparams("arbitrary"),
        name="moe_experts",
    )(tile_start, tiles_e, xs, w_gate, w_up, w_down)


COMBINE_ROWS = 64


def _combine_kernel(pos_hbm, y_hbm, base_ref, gate_ref, g_ref, b_ref, op_ref, os_ref,
                    pos_s, ybuf, gb_ref, f_ref, sem_pos, sem_row, *, n_prompt_tiles):
    i = pl.program_id(0)
    tm, d = base_ref.shape
    get_pos = pltpu.make_async_copy(pos_hbm.at[i], pos_s, sem_pos)
    get_pos.start()
    get_pos.wait()

    def row_copy(r, k):
        return pltpu.make_async_copy(_packed_rows(y_hbm, pos_s[k, r]), _packed_rows(ybuf, k * tm + r), sem_row)

    def issue(r, c):
        for k in range(TOP_K):
            row_copy(r, k).start(priority=k % 2)
        return c

    def drain(r, c):
        for k in range(TOP_K):
            row_copy(r, k).wait()
        return c

    lax.fori_loop(0, tm, issue, 0, unroll=DMA_UNROLL)
    gates = gate_ref[...]
    for k in range(TOP_K):
        gb_ref[k] = jnp.broadcast_to(gates[:, k:k + 1], (tm, LANES))
    lax.fori_loop(0, tm, drain, 0, unroll=DMA_UNROLL)

    for r0 in range(0, tm, COMBINE_ROWS):
        nr = min(COMBINE_ROWS, tm - r0)
        for s in range(PACK_ROWS):
            c_lo, c_hi = s * LANES, (PACK_ROWS + s) * LANES
            lo = base_ref[r0:r0 + nr, c_lo:c_lo + LANES]
            hi = base_ref[r0:r0 + nr, c_hi:c_hi + LANES]
            for k in range(TOP_K):
                w_lo, w_hi = _word_halves(ybuf[pl.ds((k * tm + r0) * PACK_ROWS + s, nr, stride=PACK_ROWS), :])
                g = gb_ref[k, r0:r0 + nr, :]
                lo = lo + g * w_lo
                hi = hi + g * w_hi
            f_ref[r0:r0 + nr, c_lo:c_lo + LANES] = lo
            f_ref[r0:r0 + nr, c_hi:c_hi + LANES] = hi
    y = _layer_norm(f_ref[...], g_ref[...], b_ref[...])

    @pl.when(i < n_prompt_tiles)
    def _():
        op_ref[...] = y

    @pl.when(i >= n_prompt_tiles)
    def _():
        os_ref[...] = y


def _combine(pos3, y_sorted, base, gates, ln_g, ln_b, tp, tm):
    t, d = base.shape
    n_p = tp // tm
    vec = pl.BlockSpec((1, d), lambda i: (0, 0))
    return pl.pallas_call(
        functools.partial(_combine_kernel, n_prompt_tiles=n_p),
        grid=(t // tm,),
        in_specs=[pl.BlockSpec(memory_space=pl.ANY), pl.BlockSpec(memory_space=pl.ANY),
                  pl.BlockSpec((tm, d), lambda i: (i, 0)), pl.BlockSpec((tm, TOP_K), lambda i: (i, 0)), vec, vec],
        out_specs=[pl.BlockSpec((tm, d), lambda i: (jnp.minimum(i, n_p - 1), 0)),
                   pl.BlockSpec((tm, d), lambda i: (jnp.maximum(i - n_p, 0), 0))],
        out_shape=[jax.ShapeDtypeStruct((tp, d), F32), jax.ShapeDtypeStruct((t - tp, d), F32)],
        scratch_shapes=[pltpu.SMEM((TOP_K, tm), jnp.int32),
                        pltpu.VMEM((TOP_K * tm * PACK_ROWS, LANES), jnp.uint32),
                        pltpu.VMEM((TOP_K, tm, LANES), F32),
                        pltpu.VMEM((tm, d), F32),
                        pltpu.SemaphoreType.DMA(()), pltpu.SemaphoreType.DMA(())],
        compiler_params=_cparams("arbitrary"),
        name="combine_ln2",
    )(pos3, y_sorted, base, gates, ln_g, ln_b)


def kernel(x_prompt, x_sample, cache_win_k, cache_win_v, state_conv, w_in, attn_sinks, conv_w, conv_b, conv_ln_g, conv_ln_b, attn_out_g, conv_out_g, w_out, ln1_g, ln1_b, router_w, router_b, exp_w_gate, exp_w_up, exp_w_down, sh_w_gate, sh_w_up, sh_w_down, ln2_g, ln2_b):
    assert w_in.shape[0] == DEPTH
    batch, seq, d = x_prompt.shape
    dec_batch, dec_seq, _ = x_sample.shape
    lc = cache_win_k.shape[2]
    ch = conv_w.shape[2]
    ne = router_w.shape[2]
    tp, ts = batch * seq, dec_batch * dec_seq
    t_all = tp + ts
    assert d == 2 * PACK_ROWS * LANES
    row2 = lambda a: a.reshape(1, a.shape[-1])

    w_in_b = w_in.reshape(w_in.shape[1:]).astype(BF16)
    w_out_b = w_out.reshape(w_out.shape[1:]).astype(BF16)
    rwt_b = router_w.reshape(router_w.shape[1:]).T.astype(BF16)
    sg_b = sh_w_gate.reshape(sh_w_gate.shape[1:]).astype(BF16)
    su_b = sh_w_up.reshape(sh_w_up.shape[1:]).astype(BF16)
    sd_b = sh_w_down.reshape(sh_w_down.shape[1:]).astype(BF16)
    sinks = attn_sinks.reshape(1, N_Q_HEADS)
    cw = conv_w.reshape(CONV_WIDTH, ch)
    cb, clg, clb = row2(conv_b), row2(conv_ln_g), row2(conv_ln_b)

    xp = x_prompt.reshape(tp, d)
    xs_in = x_sample.reshape(ts, d)
    cos_p, sin_p = _rope_tables(jnp.arange(seq))
    cos_s, sin_s = _rope_tables(jnp.tile(PAST_LEN + jnp.arange(dec_seq), dec_batch))

    assert NEW_PAD % dec_seq == 0 and dec_batch % SAMPLE_BB == 0 and seq % WINDOW == 0
    tm_p, tm_s = min(ROW_TILE, seq), min(ROW_TILE, ts)
    tm = min(tm_p, tm_s) // 2
    tn = min(ROUTE_TILE, 2 * tm)
    assert tp % tm == 0 and ts % tm == 0 and t_all % tn == 0
    qp, kp, vp, gp = _in_proj(xp, w_in_b, cos_p, sin_p, tm_p)
    attn_p = _prompt_attn(qp, kp, vp, sinks, batch, seq)
    conv_p = _conv_prompt(gp, cw, cb, clg, clb, batch, seq, WINDOW)
    qs, ks, vs, gs = _in_proj(xs_in, w_in_b, cos_s, sin_s, tm_s)
    ck = cache_win_k.reshape(dec_batch, lc, KV_WIDTH)
    cv = cache_win_v.reshape(dec_batch, lc, KV_WIDTH)
    attn_s = _sample_attn(qs, ks, vs, sinks, ck, cv, dec_batch, dec_seq)
    ext_s = jnp.concatenate([state_conv.reshape(dec_batch, CONV_WIDTH - 1, ch), gs.reshape(dec_batch, dec_seq, ch)], axis=1)
    conv_s = _conv_sample(ext_s, cw, cb, clg, clb, dec_seq).reshape(ts, ch)

    base, hbp, sct = _mix(attn_p, attn_s, conv_p, conv_s, xp, xs_in, w_out_b, row2(attn_out_g), row2(conv_out_g),
                          row2(ln1_g), row2(ln1_b), rwt_b, sg_b, su_b, sd_b, tm)
    bias_l = jnp.broadcast_to(router_b.reshape(ne, 1).astype(F32), (ne, LANES))
    eidx_t, gate_t, rank_t, cnt = _route(sct, bias_l, tn)

    et = EXPERT_TILE
    n_tiles = t_all * TOP_K // et + ne
    counts = cnt[:, 0]
    tiles_e = (counts + et - 1) // et
    tile_end = jnp.cumsum(tiles_e)
    tile_start = tile_end - tiles_e
    n_active = tile_end[-1]
    last_tile_row = jnp.where(tiles_e > 0, (tile_end - 1) * et, -1)
    fill_plan = jnp.concatenate([last_tile_row, n_active.reshape(1)]).astype(jnp.int32)

    pos3 = _positions((tile_start * et).astype(jnp.int32), eidx_t, rank_t, tm)
    xs = _dispatch(fill_plan, pos3, hbp, n_tiles * et, tm, et)
    wg = exp_w_gate.reshape(exp_w_gate.shape[1:])
    wu = exp_w_up.reshape(exp_w_up.shape[1:])
    wd = exp_w_down.reshape(exp_w_down.shape[1:])
    y_sorted = _moe(xs, wg, wu, wd, tile_start.astype(jnp.int32), tiles_e.astype(jnp.int32), et)
    y_p, y_s = _combine(pos3, y_sorted, base, gate_t.T, row2(ln2_g), row2(ln2_b), tp, tm)

    y_p = y_p.reshape(batch, seq, d)
    y_s = y_s.reshape(dec_batch, dec_seq, d)
    keep = min(WINDOW, seq)
    kv5 = lambda a, b: a.reshape(1, b, -1, N_KV_HEADS, HEAD_DIM)
    new_k_p = kv5(kp.reshape(batch, seq, KV_WIDTH)[:, seq - keep:], batch)
    new_v_p = kv5(vp.reshape(batch, seq, KV_WIDTH)[:, seq - keep:], batch)
    new_c_p = gp.reshape(batch, seq, ch)[:, seq - (CONV_WIDTH - 1):].reshape(1, batch, CONV_WIDTH - 1, ch)
    keep_s = min(WINDOW, PAST_LEN + dec_seq)
    k_all = jnp.concatenate([ck, ks.reshape(dec_batch, dec_seq, KV_WIDTH)], axis=1)
    v_all = jnp.concatenate([cv, vs.reshape(dec_batch, dec_seq, KV_WIDTH)], axis=1)
    new_k_s = kv5(k_all[:, k_all.shape[1] - keep_s:], dec_batch)
    new_v_s = kv5(v_all[:, v_all.shape[1] - keep_s:], dec_batch)
    new_c_s = ext_s[:, dec_seq:].reshape(1, dec_batch, CONV_WIDTH - 1, ch)
    return (y_p, y_s, new_k_p, new_v_p, new_c_p, new_k_s, new_v_s, new_c_s)
```

```python
import functools

import jax
import jax.numpy as jnp
from jax import lax
from jax.experimental import pallas as pl
from jax.experimental.pallas import tpu as pltpu

F32 = jnp.float32
BF16 = jnp.bfloat16

DEPTH = 1
PAST_LEN = 16384
HEAD_DIM = 64
N_Q_HEADS = 16
N_KV_HEADS = 4
Q_PER_KV = N_Q_HEADS // N_KV_HEADS
ATTN_WIDTH = N_Q_HEADS * HEAD_DIM
KV_WIDTH = N_KV_HEADS * HEAD_DIM
WINDOW = 128
ROPE_THETA = 10000.0
CONV_WIDTH = 31
CONV_HALO = 32
TOP_K = 8
N_EXPERT_GROUPS = 8
TOPK_GROUPS = 4
ROUTED_SCALE = 2.5
LN_EPS = 1e-5
RMS_EPS = 1e-6
DEEPNORM_ALPHA = (2.0 * DEPTH) ** 0.25

LANES = 128
VMEM_LIMIT = 56 * 1024 * 1024
ROW_TILE = 512
EXPERT_TILE = 128
PACK_ROWS = 8
ROUTE_TILE = 512
DMA_UNROLL = 8


def _cparams(*sem):
    return pltpu.CompilerParams(dimension_semantics=sem, vmem_limit_bytes=VMEM_LIMIT)


def _dot(a, b):
    return jnp.dot(a, b, preferred_element_type=F32)


def _layer_norm(x, g, b):
    xc = x - jnp.mean(x, axis=-1, keepdims=True)
    var = jnp.mean(xc * xc, axis=-1, keepdims=True)
    return xc * lax.rsqrt(var + LN_EPS) * g + b


def _rms_norm(x, g):
    ms = jnp.mean(x * x, axis=-1, keepdims=True)
    return x * lax.rsqrt(ms + RMS_EPS) * g


def _sigmoid(x):
    return 1.0 / (1.0 + jnp.exp(-x))


def _rope_tables(pos):
    half = HEAD_DIM // 2
    inv_freq = 1.0 / (ROPE_THETA ** (jnp.arange(half, dtype=F32) * (2.0 / HEAD_DIM)))
    ang = pos.astype(F32)[:, None] * inv_freq[None, :]
    cos, sin = jnp.cos(ang), jnp.sin(ang)
    reps = LANES // HEAD_DIM
    return jnp.tile(cos, (1, 2 * reps)), jnp.tile(jnp.concatenate([-sin, sin], axis=1), (1, reps))


def _in_proj_kernel(x_ref, w_ref, cos_ref, sin_ref, q_ref, k_ref, v_ref, g_ref):
    xb = x_ref[...].astype(BF16)
    cos = cos_ref[...]
    sin = sin_ref[...]
    lane = lax.broadcasted_iota(jnp.int32, cos.shape, 1)
    first_half = (lane % HEAD_DIM) < (HEAD_DIM // 2)

    def rope(acc):
        outs = []
        for c in range(acc.shape[1] // LANES):
            a = acc[:, c * LANES:(c + 1) * LANES]
            partner = jnp.where(first_half, pltpu.roll(a, LANES - HEAD_DIM // 2, 1),
                                pltpu.roll(a, HEAD_DIM // 2, 1))
            outs.append(a * cos + partner * sin)
        return jnp.concatenate(outs, axis=1)

    o_k = ATTN_WIDTH
    o_v = o_k + KV_WIDTH
    o_a = o_v + KV_WIDTH
    ch = (w_ref.shape[1] - o_a) // 2
    q = rope(_dot(xb, w_ref[:, 0:o_k]))
    q_ref[...] = (q * (HEAD_DIM ** -0.5)).astype(BF16)
    k_ref[...] = rope(_dot(xb, w_ref[:, o_k:o_v]))
    v_ref[...] = _dot(xb, w_ref[:, o_v:o_a])
    a = _dot(xb, w_ref[:, o_a:o_a + ch])
    gate = _dot(xb, w_ref[:, o_a + ch:o_a + 2 * ch])
    g_ref[...] = a * _sigmoid(gate)


def _in_proj(x2d, w_bf16, cos, sin, tm):
    t, d = x2d.shape
    n = w_bf16.shape[1]
    ch = (n - ATTN_WIDTH - 2 * KV_WIDTH) // 2
    ntab = cos.shape[0] // tm
    return pl.pallas_call(
        _in_proj_kernel,
        grid=(t // tm,),
        in_specs=[
            pl.BlockSpec((tm, d), lambda i: (i, 0)),
            pl.BlockSpec((d, n), lambda i: (0, 0), pipeline_mode=pl.Buffered(1)),
            pl.BlockSpec((tm, LANES), lambda i: (i % ntab, 0)),
            pl.BlockSpec((tm, LANES), lambda i: (i % ntab, 0)),
        ],
        out_specs=[
            pl.BlockSpec((tm, ATTN_WIDTH), lambda i: (i, 0)),
            pl.BlockSpec((tm, KV_WIDTH), lambda i: (i, 0)),
            pl.BlockSpec((tm, KV_WIDTH), lambda i: (i, 0)),
            pl.BlockSpec((tm, ch), lambda i: (i, 0)),
        ],
        out_shape=[
            jax.ShapeDtypeStruct((t, ATTN_WIDTH), BF16),
            jax.ShapeDtypeStruct((t, KV_WIDTH), F32),
            jax.ShapeDtypeStruct((t, KV_WIDTH), F32),
            jax.ShapeDtypeStruct((t, ch), F32),
        ],
        compiler_params=_cparams("parallel"),
        name="in_proj",
    )(x2d, w_bf16, cos, sin)


def _sink_softmax_pv(s, mask, sink, v_bf16):
    s = jnp.where(mask, s, -jnp.inf)
    m = jnp.maximum(jnp.max(s, axis=-1, keepdims=True), sink)
    e = jnp.exp(s - m)
    den = jnp.sum(e, axis=-1, keepdims=True) + jnp.exp(sink - m)
    return _dot(e.astype(BF16), v_bf16) / den


def _prompt_attn_kernel(sink_ref, q_ref, kc_ref, kp_ref, vc_ref, vp_ref, o_ref):
    j = pl.program_id(1)
    q = q_ref[...]
    kb = jnp.concatenate([kp_ref[...], kc_ref[...]], axis=0).astype(BF16)
    vb = jnp.concatenate([vp_ref[...], vc_ref[...]], axis=0).astype(BF16)
    r = lax.broadcasted_iota(jnp.int32, (WINDOW, 2 * WINDOW), 0)
    s_idx = lax.broadcasted_iota(jnp.int32, (WINDOW, 2 * WINDOW), 1)
    mask = (s_idx > r) & (s_idx <= r + WINDOW) & ((s_idx >= WINDOW) | (j > 0))
    outs = []
    for kv in range(N_KV_HEADS):
        kk = kb[:, kv * HEAD_DIM:(kv + 1) * HEAD_DIM]
        vv = vb[:, kv * HEAD_DIM:(kv + 1) * HEAD_DIM]
        for g in range(Q_PER_KV):
            h = kv * Q_PER_KV + g
            qh = q[:, h * HEAD_DIM:(h + 1) * HEAD_DIM]
            s = lax.dot_general(qh, kk, (((1,), (1,)), ((), ())), preferred_element_type=F32)
            outs.append(_sink_softmax_pv(s, mask, sink_ref[0, h], vv))
    o_ref[...] = jnp.concatenate(outs, axis=1).astype(o_ref.dtype)


def _prompt_attn(q, k, v, sinks, batch, seq):
    nb = seq // WINDOW
    cur = lambda b, j: (b * nb + j, 0)
    prev = lambda b, j: (b * nb + jnp.maximum(j - 1, 0), 0)
    return pl.pallas_call(
        _prompt_attn_kernel,
        grid=(batch, nb),
        in_specs=[
            pl.BlockSpec(memory_space=pltpu.SMEM),
            pl.BlockSpec((WINDOW, ATTN_WIDTH), cur),
            pl.BlockSpec((WINDOW, KV_WIDTH), cur),
            pl.BlockSpec((WINDOW, KV_WIDTH), prev),
            pl.BlockSpec((WINDOW, KV_WIDTH), cur),
            pl.BlockSpec((WINDOW, KV_WIDTH), prev),
        ],
        out_specs=pl.BlockSpec((WINDOW, ATTN_WIDTH), cur),
        out_shape=jax.ShapeDtypeStruct((batch * seq, ATTN_WIDTH), BF16),
        compiler_params=_cparams("parallel", "arbitrary"),
        name="prompt_attn",
    )(sinks, q, k, k, v, v)


SAMPLE_BB = 8
NEW_PAD = 8


def _sample_attn_kernel(sink_ref, q_ref, kn_ref, vn_ref, kc_ref, vc_ref, o_ref, *, dec_seq):
    lc = kc_ref.shape[1]
    rows = dec_seq * Q_PER_KV
    per_pad = NEW_PAD // dec_seq
    t_q = lax.broadcasted_iota(jnp.int32, (rows, lc + NEW_PAD), 0) // Q_PER_KV
    col = lax.broadcasted_iota(jnp.int32, (rows, lc + NEW_PAD), 1)
    g_idx = lax.broadcasted_iota(jnp.int32, (rows, 1), 0) % Q_PER_KV
    for bi in range(SAMPLE_BB):
        lo = (bi % per_pad) * dec_seq
        base = (bi // per_pad) * NEW_PAD
        kall = jnp.concatenate([kc_ref[bi], kn_ref[base:base + NEW_PAD, :]], axis=0).astype(BF16)
        vall = jnp.concatenate([vc_ref[bi], vn_ref[base:base + NEW_PAD, :]], axis=0).astype(BF16)
        new_t = col - lc - lo
        mask = ((col < lc) & (col + WINDOW > t_q + lc)) | ((new_t >= 0) & (new_t <= t_q))
        for kv in range(N_KV_HEADS):
            kk = kall[:, kv * HEAD_DIM:(kv + 1) * HEAD_DIM]
            vv = vall[:, kv * HEAD_DIM:(kv + 1) * HEAD_DIM]
            sink = jnp.zeros((rows, 1), F32)
            for g in range(Q_PER_KV):
                sink = jnp.where(g_idx == g, sink_ref[0, kv * Q_PER_KV + g], sink)
            s = lax.dot_general(q_ref[bi, kv], kk, (((1,), (1,)), ((), ())), preferred_element_type=F32)
            o_ref[bi, kv] = _sink_softmax_pv(s, mask, sink, vv)


def _sample_attn(q, k, v, sinks, cache_k, cache_v, dec_batch, dec_seq):
    lc = cache_k.shape[1]
    rows = dec_seq * Q_PER_KV
    q5 = q.reshape(dec_batch, dec_seq, N_KV_HEADS, Q_PER_KV, HEAD_DIM).transpose(0, 2, 1, 3, 4)
    q4 = q5.reshape(dec_batch, N_KV_HEADS, rows, HEAD_DIM)
    blk_q = pl.BlockSpec((SAMPLE_BB, N_KV_HEADS, rows, HEAD_DIM), lambda i: (i, 0, 0, 0))
    blk_new = pl.BlockSpec((SAMPLE_BB * dec_seq, KV_WIDTH), lambda i: (i, 0))
    blk_cache = pl.BlockSpec((SAMPLE_BB, lc, KV_WIDTH), lambda i: (i, 0, 0))
    o = pl.pallas_call(
        functools.partial(_sample_attn_kernel, dec_seq=dec_seq),
        grid=(dec_batch // SAMPLE_BB,),
        in_specs=[pl.BlockSpec(memory_space=pltpu.SMEM), blk_q, blk_new, blk_new, blk_cache, blk_cache],
        out_specs=blk_q,
        out_shape=jax.ShapeDtypeStruct(q4.shape, F32),
        compiler_params=_cparams("parallel"),
        name="sample_attn",
    )(sinks, q4, k, v, cache_k, cache_v)
    o = o.reshape(dec_batch, N_KV_HEADS, dec_seq, Q_PER_KV, HEAD_DIM).transpose(0, 2, 1, 3, 4)
    return o.reshape(dec_batch * dec_seq, ATTN_WIDTH).astype(BF16)


CONV_ROWS = 32
CONV_ROWS_GROUPED = 64


def _conv_taps(ext_ref, w_ref, y_ref, n_out, first_row):
    ch = y_ref.shape[-1]
    for r0 in range(0, n_out, CONV_ROWS):
        nr = min(CONV_ROWS, n_out - r0)
        for c0 in range(0, ch, LANES):
            acc = jnp.zeros((nr, LANES), F32)
            for j in range(CONV_WIDTH):
                acc = acc + ext_ref[pl.ds(first_row + r0 + j, nr), c0:c0 + LANES] * w_ref[j:j + 1, c0:c0 + LANES]
            y_ref[r0:r0 + nr, c0:c0 + LANES] = acc


def _conv_taps_grouped(ext_ref, w_ref, y_ref, n_out, first_row):
    ch = y_ref.shape[-1]
    sub = 8
    groups = [[j for j in range(CONV_WIDTH) if (first_row + j) % sub == res] for res in range(sub)]
    for r0 in range(0, n_out, CONV_ROWS_GROUPED):
        nr = min(CONV_ROWS_GROUPED, n_out - r0)
        for c0 in range(0, ch, LANES):
            y = jnp.zeros((nr, LANES), F32)
            for res, taps in enumerate(groups):
                rows = nr if res == 0 else nr + sub
                acc = jnp.zeros((rows, LANES), F32)
                for j in taps:
                    acc = acc + ext_ref[pl.ds(r0 + first_row + j - res, rows), c0:c0 + LANES] * w_ref[j:j + 1, c0:c0 + LANES]
                if taps:
                    y = y + acc[res:res + nr, :]
            y_ref[r0:r0 + nr, c0:c0 + LANES] = y


def _conv_post(y, b_ref, g_ref, beta_ref):
    z = _layer_norm(y + b_ref[...], g_ref[...], beta_ref[...])
    return z * _sigmoid(z)


def _conv_prompt_kernel(cur_ref, halo_ref, w_ref, b_ref, g_ref, beta_ref, o_ref, ext_ref, y_ref):
    i = pl.program_id(1)
    tt = cur_ref.shape[0]
    ext_ref[0:CONV_HALO, :] = jnp.where(i > 0, halo_ref[...], 0.0)
    ext_ref[CONV_HALO:CONV_HALO + tt, :] = cur_ref[...]
    _conv_taps_grouped(ext_ref, w_ref, y_ref, tt, CONV_HALO - (CONV_WIDTH - 1))
    o_ref[...] = _conv_post(y_ref[...], b_ref, g_ref, beta_ref).astype(o_ref.dtype)


def _conv_prompt(g, w, b, ln_g, ln_b, batch, seq, tt):
    ch = g.shape[1]
    nt = seq // tt
    hb = tt // CONV_HALO
    vec = pl.BlockSpec((1, ch), lambda bi, i: (0, 0))
    return pl.pallas_call(
        _conv_prompt_kernel,
        grid=(batch, nt),
        in_specs=[
            pl.BlockSpec((tt, ch), lambda bi, i: (bi * nt + i, 0)),
            pl.BlockSpec((CONV_HALO, ch), lambda bi, i: (jnp.maximum((bi * nt + i) * hb - 1, 0), 0)),
            pl.BlockSpec((CONV_WIDTH, ch), lambda bi, i: (0, 0)),
            vec, vec, vec,
        ],
        out_specs=pl.BlockSpec((tt, ch), lambda bi, i: (bi * nt + i, 0)),
        out_shape=jax.ShapeDtypeStruct(g.shape, BF16),
        scratch_shapes=[pltpu.VMEM((CONV_HALO + tt, ch), F32), pltpu.VMEM((tt, ch), F32)],
        compiler_params=_cparams("parallel", "arbitrary"),
        name="conv_prompt",
    )(g, g, w, b, ln_g, ln_b)


def _conv_sample_kernel(ext_ref, w_ref, b_ref, g_ref, beta_ref, o_ref, y_ref):
    nb, _, ch = ext_ref.shape
    dec = o_ref.shape[1]
    for bi in range(nb):
        _conv_taps(ext_ref.at[bi], w_ref, y_ref.at[bi], dec, 0)
    for bi in range(nb):
        o_ref[bi] = _conv_post(y_ref[bi], b_ref, g_ref, beta_ref)


def _conv_sample(ext, w, b, ln_g, ln_b, dec_seq):
    nb, rows, ch = ext.shape
    vec = pl.BlockSpec((1, ch), lambda i: (0, 0))
    return pl.pallas_call(
        _conv_sample_kernel,
        grid=(nb // SAMPLE_BB,),
        in_specs=[
            pl.BlockSpec((SAMPLE_BB, rows, ch), lambda i: (i, 0, 0)),
            pl.BlockSpec((CONV_WIDTH, ch), lambda i: (0, 0)),
            vec, vec, vec,
        ],
        out_specs=pl.BlockSpec((SAMPLE_BB, dec_seq, ch), lambda i: (i, 0, 0)),
        out_shape=jax.ShapeDtypeStruct((nb, dec_seq, ch), F32),
        scratch_shapes=[pltpu.VMEM((SAMPLE_BB, dec_seq, ch), F32)],
        compiler_params=_cparams("parallel"),
        name="conv_sample",
    )(ext, w, b, ln_g, ln_b)


def _bf16_words(vals):
    half = vals.shape[1] // 2
    bits = lax.bitcast_convert_type(vals, jnp.uint32)
    return (bits[:, :half] >> 16) | (bits[:, half:] & jnp.uint32(0xFFFF0000))


def _word_halves(words):
    lo = lax.bitcast_convert_type(words << 16, F32)
    hi = lax.bitcast_convert_type(words & jnp.uint32(0xFFFF0000), F32)
    return lo, hi


def _store_packed(ref, words, base=0):
    n = words.shape[0]
    for s in range(PACK_ROWS):
        ref[pl.ds(base + s, n, stride=PACK_ROWS), :] = words[:, s * LANES:(s + 1) * LANES]


def _mix_kernel(ap_ref, as_ref, cp_ref, cs_ref, xp_ref, xs_ref, wo_ref, ag_ref, cg_ref, g1_ref, b1_ref, rwt_ref,
                sg_ref, su_ref, sd_ref, base_ref, hbp_ref, sct_ref, *, n_prompt_tiles):
    is_p = pl.program_id(0) < n_prompt_tiles
    attn = jnp.where(is_p, ap_ref[...], as_ref[...]).astype(F32)
    conv = jnp.where(is_p, cp_ref[...], cs_ref[...]).astype(F32)
    x = jnp.where(is_p, xp_ref[...], xs_ref[...])
    aw = attn.shape[1]
    a = _rms_norm(attn, ag_ref[...]).astype(BF16)
    c = _rms_norm(conv, cg_ref[...]).astype(BF16)
    mixed = _dot(a, wo_ref[0:aw, :]) + _dot(c, wo_ref[aw:, :])
    h = _layer_norm(DEEPNORM_ALPHA * x + mixed, g1_ref[...], b1_ref[...])
    hb = h.astype(BF16)
    _store_packed(hbp_ref, _bf16_words(hb.astype(F32)))
    logits_t = lax.dot_general(rwt_ref[...], hb, (((1,), (1,)), ((), ())), preferred_element_type=F32)
    sct_ref[...] = _sigmoid(logits_t)
    h1 = _dot(hb, sg_ref[...])
    act = (h1 * _sigmoid(h1) * _dot(hb, su_ref[...])).astype(BF16)
    base_ref[...] = DEEPNORM_ALPHA * h + _dot(act, sd_ref[...])


def _mix(attn_p, attn_s, conv_p, conv_s, xp, xs, wo, ag, cg, g1, b1, rwt, sg, su, sd, tm):
    tp, d = xp.shape
    ts = xs.shape[0]
    n_p, n_s = tp // tm, ts // tm
    ne = rwt.shape[0]
    t = tp + ts
    prow = lambda w: pl.BlockSpec((tm, w), lambda i: (jnp.minimum(i, n_p - 1), 0))
    srow = lambda w: pl.BlockSpec((tm, w), lambda i: (jnp.maximum(i - n_p, 0), 0))
    full = lambda a: pl.BlockSpec(a.shape, lambda i: (0,) * a.ndim, pipeline_mode=pl.Buffered(1))
    aw, cw = attn_p.shape[1], conv_p.shape[1]
    return pl.pallas_call(
        functools.partial(_mix_kernel, n_prompt_tiles=n_p),
        grid=(n_p + n_s,),
        in_specs=[prow(aw), srow(aw), prow(cw), srow(cw), prow(d), srow(d)]
        + [full(a) for a in (wo, ag, cg, g1, b1, rwt, sg, su, sd)],
        out_specs=[pl.BlockSpec((tm, d), lambda i: (i, 0)),
                   pl.BlockSpec((tm * PACK_ROWS, LANES), lambda i: (i, 0)),
                   pl.BlockSpec((ne, tm), lambda i: (0, i))],
        out_shape=[jax.ShapeDtypeStruct((t, d), F32),
                   jax.ShapeDtypeStruct((t * PACK_ROWS, LANES), jnp.uint32),
                   jax.ShapeDtypeStruct((ne, t), F32)],
        compiler_params=_cparams("arbitrary"),
        name="mix_ln1",
    )(attn_p, attn_s, conv_p, conv_s, xp, xs, wo, ag, cg, g1, b1, rwt, sg, su, sd)


def _route_kernel(sc_ref, bias_ref, tri_ref, eidx_ref, gate_ref, rank_ref, cnt_ref, carry_ref, mask_ref, eidf_ref):
    ne, tn = sc_ref.shape
    per = ne // N_EXPERT_GROUPS
    ninf = -jnp.inf

    @pl.when(pl.program_id(0) == 0)
    def _():
        carry_ref[...] = jnp.zeros_like(carry_ref)

    erow = lax.broadcasted_iota(jnp.int32, (ne, LANES), 0).astype(F32)
    grow = lax.broadcasted_iota(jnp.int32, (per, LANES), 0).astype(F32)
    gi = lax.broadcasted_iota(jnp.int32, (N_EXPERT_GROUPS, LANES), 0).astype(F32)
    colmax = lambda v: jnp.max(v, axis=0, keepdims=True)
    first_at = lambda v, m, idx, n: jnp.min(jnp.where(v == m, idx, float(n)), axis=0, keepdims=True)

    for c in range(tn // LANES):
        lanes = slice(c * LANES, (c + 1) * LANES)
        sc = sc_ref[:, lanes]
        sel = sc + bias_ref[...]
        gs = []
        for g in range(N_EXPERT_GROUPS):
            v = sel[g * per:(g + 1) * per, :]
            m1 = colmax(v)
            m2 = colmax(jnp.where(grow == first_at(v, m1, grow, per), ninf, v))
            gs.append(m1 + m2)
        grp = jnp.concatenate(gs, axis=0)
        gsel = jnp.zeros_like(grp)
        for _ in range(TOPK_GROUPS):
            m = colmax(grp)
            hit = gi == first_at(grp, m, gi, N_EXPERT_GROUPS)
            gsel = jnp.where(hit, 1.0, gsel)
            grp = jnp.where(hit, ninf, grp)
        emask = jnp.concatenate([jnp.broadcast_to(gsel[g:g + 1, :], (per, LANES)) for g in range(N_EXPERT_GROUPS)], axis=0)
        masked = jnp.where(emask > 0.5, sel, ninf)
        selm = jnp.zeros_like(sel)
        idxs, gts = [], []
        for _ in range(TOP_K):
            ix = first_at(masked, colmax(masked), erow, ne)
            hit = erow == ix
            gts.append(jnp.sum(jnp.where(hit, sc, 0.0), axis=0, keepdims=True))
            idxs.append(ix)
            masked = jnp.where(hit, ninf, masked)
            selm = jnp.where(hit, 1.0, selm)
        gsum = gts[0]
        for g in gts[1:]:
            gsum = gsum + g
        eidf = jnp.concatenate(idxs, axis=0)
        eidf_ref[:, lanes] = eidf
        eidx_ref[:, lanes] = eidf.astype(jnp.int32)
        gate_ref[:, lanes] = jnp.concatenate([g / gsum * ROUTED_SCALE for g in gts], axis=0)
        mask_ref[:, lanes] = selm

    selm_all = mask_ref[...]
    incl = _dot(selm_all.astype(BF16), tri_ref[...])
    carry = carry_ref[...]
    for c in range(tn // LANES):
        lanes = slice(c * LANES, (c + 1) * LANES)
        before = incl[:, lanes] - selm_all[:, lanes] + carry
        eidf = eidf_ref[:, lanes]
        rk = [jnp.sum(jnp.where(erow == eidf[k:k + 1, :], before, 0.0), axis=0, keepdims=True) for k in range(TOP_K)]
        rank_ref[:, lanes] = jnp.concatenate(rk, axis=0).astype(jnp.int32)
    total = carry + jnp.broadcast_to(incl[:, tn - 1:tn], (ne, LANES))
    carry_ref[...] = total
    cnt_ref[...] = total.astype(jnp.int32)


def _route(sct, bias_l, tn):
    ne, t = sct.shape
    tri = jnp.triu(jnp.ones((tn, tn), BF16))
    col = pl.BlockSpec((TOP_K, tn), lambda i: (0, i))
    return pl.pallas_call(
        _route_kernel,
        grid=(t // tn,),
        in_specs=[pl.BlockSpec((ne, tn), lambda i: (0, i)),
                  pl.BlockSpec((ne, LANES), lambda i: (0, 0)),
                  pl.BlockSpec((tn, tn), lambda i: (0, 0))],
        out_specs=[col, col, col, pl.BlockSpec((ne, LANES), lambda i: (0, 0))],
        out_shape=[jax.ShapeDtypeStruct((TOP_K, t), jnp.int32), jax.ShapeDtypeStruct((TOP_K, t), F32),
                   jax.ShapeDtypeStruct((TOP_K, t), jnp.int32), jax.ShapeDtypeStruct((ne, LANES), jnp.int32)],
        scratch_shapes=[pltpu.VMEM((ne, LANES), F32), pltpu.VMEM((ne, tn), F32), pltpu.VMEM((TOP_K, tn), F32)],
        compiler_params=_cparams("arbitrary"),
        name="route",
    )(sct, bias_l, tri)


def _pos_kernel(pstart_ref, eidx_ref, rank_ref, pos_ref):
    eid = eidx_ref[...]
    base = lax.fori_loop(0, pstart_ref.shape[0], lambda e, acc: jnp.where(eid == e, pstart_ref[e], acc),
                         jnp.zeros_like(eid), unroll=DMA_UNROLL)
    pos_ref[...] = base + rank_ref[...]


def _positions(pstart, eidx_t, rank_t, tm):
    t = eidx_t.shape[1]
    col = pl.BlockSpec((TOP_K, tm), lambda i, ps: (0, i))
    return pl.pallas_call(
        _pos_kernel,
        grid_spec=pltpu.PrefetchScalarGridSpec(
            num_scalar_prefetch=1, grid=(t // tm,), in_specs=[col, col],
            out_specs=pl.BlockSpec((None, TOP_K, tm), lambda i, ps: (i, 0, 0))),
        out_shape=jax.ShapeDtypeStruct((t // tm, TOP_K, tm), jnp.int32),
        compiler_params=_cparams("parallel"),
        name="positions",
    )(pstart, eidx_t, rank_t)


def _packed_rows(ref, row):
    return ref.at[pl.ds(pl.multiple_of(row * PACK_ROWS, PACK_ROWS), PACK_ROWS), :]


def _dispatch_kernel(last_ref, pos_hbm, hb_ref, xs_hbm, pos_s, zero_ref, sem_pos, sem_zero, sem_row):
    i = pl.program_id(0)
    tm = pos_s.shape[1]
    tile_rows = zero_ref.shape[0]

    @pl.when(i == 0)
    def _():
        zero_ref[...] = jnp.zeros_like(zero_ref)

        def fill(e):
            start = pl.multiple_of(last_ref[e] * PACK_ROWS, PACK_ROWS)
            return pltpu.make_async_copy(zero_ref, xs_hbm.at[pl.ds(start, tile_rows), :], sem_zero)

        def go(e, c):
            @pl.when(last_ref[e] >= 0)
            def _():
                fill(e).start()
            return c

        def done(e, c):
            @pl.when(last_ref[e] >= 0)
            def _():
                fill(e).wait()
            return c

        def tail(t):
            start = pl.multiple_of(t * tile_rows, PACK_ROWS)
            return pltpu.make_async_copy(zero_ref, xs_hbm.at[pl.ds(start, tile_rows), :], sem_zero)

        def tail_go(t, c):
            tail(t).start()
            return c

        def tail_done(t, c):
            tail(t).wait()
            return c

        n_experts = last_ref.shape[0] - 1
        n_active, n_tiles = last_ref[n_experts], xs_hbm.shape[0] // tile_rows
        lax.fori_loop(0, n_experts, go, 0)
        lax.fori_loop(n_active, n_tiles, tail_go, 0)
        lax.fori_loop(0, n_experts, done, 0)
        lax.fori_loop(n_active, n_tiles, tail_done, 0)

    get_pos = pltpu.make_async_copy(pos_hbm.at[i], pos_s, sem_pos)
    get_pos.start()
    get_pos.wait()

    def row_copy(r, k):
        return pltpu.make_async_copy(_packed_rows(hb_ref, r), _packed_rows(xs_hbm, pos_s[k, r]), sem_row)

    def issue(r, c):
        for k in range(TOP_K):
            row_copy(r, k).start(priority=k % 2)
        return c

    def drain(r, c):
        for k in range(TOP_K):
            row_copy(r, k).wait()
        return c

    lax.fori_loop(0, tm, issue, 0, unroll=DMA_UNROLL)
    lax.fori_loop(0, tm, drain, 0, unroll=DMA_UNROLL)


def _dispatch(fill_plan, pos3, hbp, n_rows, tm, expert_tile):
    n_tok_tiles = pos3.shape[0]
    return pl.pallas_call(
        _dispatch_kernel,
        grid_spec=pltpu.PrefetchScalarGridSpec(
            num_scalar_prefetch=1, grid=(n_tok_tiles,),
            in_specs=[pl.BlockSpec(memory_space=pl.ANY),
                      pl.BlockSpec((tm * PACK_ROWS, LANES), lambda i, lt: (i, 0))],
            out_specs=pl.BlockSpec(memory_space=pl.ANY),
            scratch_shapes=[pltpu.SMEM((TOP_K, tm), jnp.int32),
                            pltpu.VMEM((expert_tile * PACK_ROWS, LANES), jnp.uint32),
                            pltpu.SemaphoreType.DMA(()), pltpu.SemaphoreType.DMA(()), pltpu.SemaphoreType.DMA(())]),
        out_shape=jax.ShapeDtypeStruct((n_rows * PACK_ROWS, LANES), jnp.uint32),
        compiler_params=_cparams("arbitrary"),
        name="dispatch",
    )(fill_plan, pos3, hbp)


def _moe_kernel(first_ref, count_ref, x_hbm, wg_ref, wu_ref, wd_ref, y_hbm, xbuf, ybuf, xrem, yrem, wgb, wub, wdb,
                pend, sem_x, sem_y, sem_r):
    e = pl.program_id(0)
    n, t0 = count_ref[e], first_ref[e]
    tile_rows = xrem.shape[0]
    pair_rows = 2 * tile_rows
    n_pairs = n // 2
    odd = n - 2 * n_pairs

    def hbm_rows(ref, tile, rows):
        return ref.at[pl.ds(pl.multiple_of(tile * tile_rows, PACK_ROWS), rows), :]

    def slot_rows(ref, slot):
        return ref.at[pl.ds(pl.multiple_of(slot * pair_rows, PACK_ROWS), pair_rows), :]

    def x_copy(j, slot):
        return pltpu.make_async_copy(hbm_rows(x_hbm, t0 + 2 * j, pair_rows), slot_rows(xbuf, slot), sem_x.at[slot])

    def y_copy(first, j, slot):
        return pltpu.make_async_copy(slot_rows(ybuf, slot), hbm_rows(y_hbm, first + 2 * j, pair_rows), sem_y.at[slot])

    def y_last(first, pairs):
        return pltpu.make_async_copy(yrem, hbm_rows(y_hbm, first + 2 * pairs, tile_rows), sem_r.at[1])

    x_last = pltpu.make_async_copy(hbm_rows(x_hbm, t0 + 2 * n_pairs, tile_rows), xrem, sem_r.at[0])

    def drain_previous():
        first, count = pend[1], pend[2]
        pairs = count // 2

        @pl.when(pairs >= 2)
        def _():
            y_copy(first, pairs - 2, pairs % 2).wait()

        @pl.when(pairs >= 1)
        def _():
            y_copy(first, pairs - 1, (pairs - 1) % 2).wait()

        @pl.when(count - 2 * pairs == 1)
        def _():
            y_last(first, pairs).wait()

        pend[0] = 0

    def experts_mlp(src, src_base, dst, dst_base, rows):
        tm = rows // PACK_ROWS
        halves = [_word_halves(src[pl.ds(src_base + s, tm, stride=PACK_ROWS), :]) for s in range(PACK_ROWS)]
        x = jnp.concatenate([h[0] for h in halves] + [h[1] for h in halves], axis=1).astype(BF16)
        h1 = _dot(x, wgb[...])
        act = (h1 * _sigmoid(h1) * _dot(x, wub[...])).astype(BF16)
        y = _dot(act, wdb[...]).astype(BF16)
        _store_packed(dst, _bf16_words(y.astype(F32)), dst_base)

    @pl.when(e == 0)
    def _():
        pend[0] = 0

    @pl.when(n > 0)
    def _():
        @pl.when(n_pairs > 0)
        def _():
            x_copy(0, 0).start()

        @pl.when(odd == 1)
        def _():
            x_last.start()

        wgb[...] = wg_ref[...].astype(BF16)
        wub[...] = wu_ref[...].astype(BF16)
        wdb[...] = wd_ref[...].astype(BF16)

        @pl.when(pend[0] == 1)
        def _():
            drain_previous()

        @pl.when(odd == 1)
        def _():
            x_last.wait()
            experts_mlp(xrem, 0, yrem, 0, tile_rows)
            y_last(t0, n_pairs).start()

        def step(j, c):
            slot = j % 2
            x_copy(j, slot).wait()

            @pl.when(j + 1 < n_pairs)
            def _():
                x_copy(j + 1, 1 - slot).start()

            @pl.when(j >= 2)
            def _():
                y_copy(t0, j - 2, slot).wait()

            experts_mlp(xbuf, slot * pair_rows, ybuf, slot * pair_rows, pair_rows)
            y_copy(t0, j, slot).start()
            return c

        lax.fori_loop(0, n_pairs, step, 0)
        pend[0] = 1
        pend[1] = t0
        pend[2] = n

    @pl.when((e == pl.num_programs(0) - 1) & (pend[0] == 1))
    def _():
        drain_previous()


def _moe(xs, w_gate, w_up, w_down, tile_start, tiles_e, tm):
    ne, d, ff = w_gate.shape
    tile_rows = tm * PACK_ROWS
    wmap = lambda e, ts, nt: (e, 0, 0)
    return pl.pallas_call(
        _moe_kernel,
        grid_spec=pltpu.PrefetchScalarGridSpec(
            num_scalar_prefetch=2,
            grid=(ne,),
            in_specs=[
                pl.BlockSpec(memory_space=pl.ANY),
                pl.BlockSpec((None, d, ff), wmap),
                pl.BlockSpec((None, d, ff), wmap),
                pl.BlockSpec((None, ff, d), wmap),
            ],
            out_specs=pl.BlockSpec(memory_space=pl.ANY),
            scratch_shapes=[pltpu.VMEM((4 * tile_rows, LANES), jnp.uint32), pltpu.VMEM((4 * tile_rows, LANES), jnp.uint32),
                            pltpu.VMEM((tile_rows, LANES), jnp.uint32), pltpu.VMEM((tile_rows, LANES), jnp.uint32),
                            pltpu.VMEM((d, ff), BF16), pltpu.VMEM((d, ff), BF16), pltpu.VMEM((ff, d), BF16),
                            pltpu.SMEM((3,), jnp.int32),
                            pltpu.SemaphoreType.DMA((2,)), pltpu.SemaphoreType.DMA((2,)), pltpu.SemaphoreType.DMA((2,))],
        ),
        out_shape=jax.ShapeDtypeStruct(xs.shape, jnp.uint32),
        input_output_aliases={2: 0},
        compiler_params=_cparams("arbitrary"),
        name="moe_experts",
    )(tile_start, tiles_e, xs, w_gate, w_up, w_down)


COMBINE_ROWS = 64


def _combine_kernel(pos_hbm, y_hbm, base_ref, gate_ref, g_ref, b_ref, op_ref, os_ref,
                    pos_s, ybuf, gb_ref, f_ref, sem_pos, sem_row, *, n_prompt_tiles):
    i = pl.program_id(0)
    tm, d = base_ref.shape
    get_pos = pltpu.make_async_copy(pos_hbm.at[i], pos_s, sem_pos)
    get_pos.start()
    get_pos.wait()

    def row_copy(r, k):
        return pltpu.make_async_copy(_packed_rows(y_hbm, pos_s[k, r]), _packed_rows(ybuf, k * tm + r), sem_row)

    def issue(r, c):
        for k in range(TOP_K):
            row_copy(r, k).start(priority=k % 2)
        return c

    def drain(r, c):
        for k in range(TOP_K):
            row_copy(r, k).wait()
        return c

    lax.fori_loop(0, tm, issue, 0, unroll=DMA_UNROLL)
    gates = gate_ref[...]
    for k in range(TOP_K):
        gb_ref[k] = jnp.broadcast_to(gates[:, k:k + 1], (tm, LANES))
    lax.fori_loop(0, tm, drain, 0, unroll=DMA_UNROLL)

    for r0 in range(0, tm, COMBINE_ROWS):
        nr = min(COMBINE_ROWS, tm - r0)
        for s in range(PACK_ROWS):
            c_lo, c_hi = s * LANES, (PACK_ROWS + s) * LANES
            lo = base_ref[r0:r0 + nr, c_lo:c_lo + LANES]
            hi = base_ref[r0:r0 + nr, c_hi:c_hi + LANES]
            for k in range(TOP_K):
                w_lo, w_hi = _word_halves(ybuf[pl.ds((k * tm + r0) * PACK_ROWS + s, nr, stride=PACK_ROWS), :])
                g = gb_ref[k, r0:r0 + nr, :]
                lo = lo + g * w_lo
                hi = hi + g * w_hi
            f_ref[r0:r0 + nr, c_lo:c_lo + LANES] = lo
            f_ref[r0:r0 + nr, c_hi:c_hi + LANES] = hi
    y = _layer_norm(f_ref[...], g_ref[...], b_ref[...])

    @pl.when(i < n_prompt_tiles)
    def _():
        op_ref[...] = y

    @pl.when(i >= n_prompt_tiles)
    def _():
        os_ref[...] = y


def _combine(pos3, y_sorted, base, gates, ln_g, ln_b, tp, tm):
    t, d = base.shape
    n_p = tp // tm
    vec = pl.BlockSpec((1, d), lambda i: (0, 0))
    return pl.pallas_call(
        functools.partial(_combine_kernel, n_prompt_tiles=n_p),
        grid=(t // tm,),
        in_specs=[pl.BlockSpec(memory_space=pl.ANY), pl.BlockSpec(memory_space=pl.ANY),
                  pl.BlockSpec((tm, d), lambda i: (i, 0)), pl.BlockSpec((tm, TOP_K), lambda i: (i, 0)), vec, vec],
        out_specs=[pl.BlockSpec((tm, d), lambda i: (jnp.minimum(i, n_p - 1), 0)),
                   pl.BlockSpec((tm, d), lambda i: (jnp.maximum(i - n_p, 0), 0))],
        out_shape=[jax.ShapeDtypeStruct((tp, d), F32), jax.ShapeDtypeStruct((t - tp, d), F32)],
        scratch_shapes=[pltpu.SMEM((TOP_K, tm), jnp.int32),
                        pltpu.VMEM((TOP_K * tm * PACK_ROWS, LANES), jnp.uint32),
                        pltpu.VMEM((TOP_K, tm, LANES), F32),
                        pltpu.VMEM((tm, d), F32),
                        pltpu.SemaphoreType.DMA(()), pltpu.SemaphoreType.DMA(())],
        compiler_params=_cparams("arbitrary"),
        name="combine_ln2",
    )(pos3, y_sorted, base, gates, ln_g, ln_b)


def kernel(x_prompt, x_sample, cache_win_k, cache_win_v, state_conv, w_in, attn_sinks, conv_w, conv_b, conv_ln_g, conv_ln_b, attn_out_g, conv_out_g, w_out, ln1_g, ln1_b, router_w, router_b, exp_w_gate, exp_w_up, exp_w_down, sh_w_gate, sh_w_up, sh_w_down, ln2_g, ln2_b):
    assert w_in.shape[0] == DEPTH
    batch, seq, d = x_prompt.shape
    dec_batch, dec_seq, _ = x_sample.shape
    lc = cache_win_k.shape[2]
    ch = conv_w.shape[2]
    ne = router_w.shape[2]
    tp, ts = batch * seq, dec_batch * dec_seq
    t_all = tp + ts
    assert d == 2 * PACK_ROWS * LANES
    row2 = lambda a: a.reshape(1, a.shape[-1])

    w_in_b = w_in.reshape(w_in.shape[1:]).astype(BF16)
    w_out_b = w_out.reshape(w_out.shape[1:]).astype(BF16)
    rwt_b = router_w.reshape(router_w.shape[1:]).T.astype(BF16)
    sg_b = sh_w_gate.reshape(sh_w_gate.shape[1:]).astype(BF16)
    su_b = sh_w_up.reshape(sh_w_up.shape[1:]).astype(BF16)
    sd_b = sh_w_down.reshape(sh_w_down.shape[1:]).astype(BF16)
    sinks = attn_sinks.reshape(1, N_Q_HEADS)
    cw = conv_w.reshape(CONV_WIDTH, ch)
    cb, clg, clb = row2(conv_b), row2(conv_ln_g), row2(conv_ln_b)

    xp = x_prompt.reshape(tp, d)
    xs_in = x_sample.reshape(ts, d)
    cos_p, sin_p = _rope_tables(jnp.arange(seq))
    cos_s, sin_s = _rope_tables(jnp.tile(PAST_LEN + jnp.arange(dec_seq), dec_batch))

    assert NEW_PAD % dec_seq == 0 and dec_batch % SAMPLE_BB == 0 and seq % WINDOW == 0
    tm_p, tm_s = min(ROW_TILE, seq), min(ROW_TILE, ts)
    tm = min(tm_p, tm_s) // 2
    tn = min(ROUTE_TILE, 2 * tm)
    assert tp % tm == 0 and ts % tm == 0 and t_all % tn == 0
    qp, kp, vp, gp = _in_proj(xp, w_in_b, cos_p, sin_p, tm_p)
    attn_p = _prompt_attn(qp, kp, vp, sinks, batch, seq)
    conv_p = _conv_prompt(gp, cw, cb, clg, clb, batch, seq, WINDOW)
    qs, ks, vs, gs = _in_proj(xs_in, w_in_b, cos_s, sin_s, tm_s)
    ck = cache_win_k.reshape(dec_batch, lc, KV_WIDTH)
    cv = cache_win_v.reshape(dec_batch, lc, KV_WIDTH)
    attn_s = _sample_attn(qs, ks, vs, sinks, ck, cv, dec_batch, dec_seq)
    ext_s = jnp.concatenate([state_conv.reshape(dec_batch, CONV_WIDTH - 1, ch), gs.reshape(dec_batch, dec_seq, ch)], axis=1)
    conv_s = _conv_sample(ext_s, cw, cb, clg, clb, dec_seq).reshape(ts, ch).astype(BF16)

    base, hbp, sct = _mix(attn_p, attn_s, conv_p, conv_s, xp, xs_in, w_out_b, row2(attn_out_g), row2(conv_out_g),
                          row2(ln1_g), row2(ln1_b), rwt_b, sg_b, su_b, sd_b, tm)
    bias_l = jnp.broadcast_to(router_b.reshape(ne, 1).astype(F32), (ne, LANES))
    eidx_t, gate_t, rank_t, cnt = _route(sct, bias_l, tn)

    et = EXPERT_TILE
    n_tiles = t_all * TOP_K // et + ne
    counts = cnt[:, 0]
    tiles_e = (counts + et - 1) // et
    tile_end = jnp.cumsum(tiles_e)
    tile_start = tile_end - tiles_e
    n_active = tile_end[-1]
    last_tile_row = jnp.where(tiles_e > 0, (tile_end - 1) * et, -1)
    fill_plan = jnp.concatenate([last_tile_row, n_active.reshape(1)]).astype(jnp.int32)

    pos3 = _positions((tile_start * et).astype(jnp.int32), eidx_t, rank_t, tm)
    xs = _dispatch(fill_plan, pos3, hbp, n_tiles * et, tm, et)
    wg = exp_w_gate.reshape(exp_w_gate.shape[1:])
    wu = exp_w_up.reshape(exp_w_up.shape[1:])
    wd = exp_w_down.reshape(exp_w_down.shape[1:])
    y_sorted = _moe(xs, wg, wu, wd, tile_start.astype(jnp.int32), tiles_e.astype(jnp.int32), et)
    y_p, y_s = _combine(pos3, y_sorted, base, gate_t.T, row2(ln2_g), row2(ln2_b), tp, tm)

    y_p = y_p.reshape(batch, seq, d)
    y_s = y_s.reshape(dec_batch, dec_seq, d)
    keep = min(WINDOW, seq)
    kv5 = lambda a, b: a.reshape(1, b, -1, N_KV_HEADS, HEAD_DIM)
    new_k_p = kv5(kp.reshape(batch, seq, KV_WIDTH)[:, seq - keep:], batch)
    new_v_p = kv5(vp.reshape(batch, seq, KV_WIDTH)[:, seq - keep:], batch)
    new_c_p = gp.reshape(batch, seq, ch)[:, seq - (CONV_WIDTH - 1):].reshape(1, batch, CONV_WIDTH - 1, ch)
    keep_s = min(WINDOW, PAST_LEN + dec_seq)
    k_all = jnp.concatenate([ck, ks.reshape(dec_batch, dec_seq, KV_WIDTH)], axis=1)
    v_all = jnp.concatenate([cv, vs.reshape(dec_batch, dec_seq, KV_WIDTH)], axis=1)
    new_k_s = kv5(k_all[:, k_all.shape[1] - keep_s:], dec_batch)
    new_v_s = kv5(v_all[:, v_all.shape[1] - keep_s:], dec_batch)
    new_c_s = ext_s[:, dec_seq:].reshape(1, dec_batch, CONV_WIDTH - 1, ch)
    return (y_p, y_s, new_k_p, new_v_p, new_c_p, new_k_s, new_v_s, new_c_s)
```

```python
import functools

import jax
import jax.numpy as jnp
from jax import lax
from jax.experimental import pallas as pl
from jax.experimental.pallas import tpu as pltpu

F32 = jnp.float32
BF16 = jnp.bfloat16

DEPTH = 1
PAST_LEN = 16384
HEAD_DIM = 64
N_Q_HEADS = 16
N_KV_HEADS = 4
Q_PER_KV = N_Q_HEADS // N_KV_HEADS
ATTN_WIDTH = N_Q_HEADS * HEAD_DIM
KV_WIDTH = N_KV_HEADS * HEAD_DIM
WINDOW = 128
ROPE_THETA = 10000.0
CONV_WIDTH = 31
CONV_HALO = 32
TOP_K = 8
N_EXPERT_GROUPS = 8
TOPK_GROUPS = 4
ROUTED_SCALE = 2.5
LN_EPS = 1e-5
RMS_EPS = 1e-6
DEEPNORM_ALPHA = (2.0 * DEPTH) ** 0.25

LANES = 128
VMEM_LIMIT = 56 * 1024 * 1024
ROW_TILE = 512
EXPERT_TILE = 128
PACK_ROWS = 8
ROUTE_TILE = 512
DMA_UNROLL = 8


def _cparams(*sem):
    return pltpu.CompilerParams(dimension_semantics=sem, vmem_limit_bytes=VMEM_LIMIT)


def _dot(a, b):
    return jnp.dot(a, b, preferred_element_type=F32)


def _layer_norm(x, g, b):
    xc = x - jnp.mean(x, axis=-1, keepdims=True)
    var = jnp.mean(xc * xc, axis=-1, keepdims=True)
    return xc * lax.rsqrt(var + LN_EPS) * g + b


def _rms_norm(x, g):
    ms = jnp.mean(x * x, axis=-1, keepdims=True)
    return x * lax.rsqrt(ms + RMS_EPS) * g


def _sigmoid(x):
    return 1.0 / (1.0 + jnp.exp(-x))


def _rope_tables(pos):
    half = HEAD_DIM // 2
    inv_freq = 1.0 / (ROPE_THETA ** (jnp.arange(half, dtype=F32) * (2.0 / HEAD_DIM)))
    ang = pos.astype(F32)[:, None] * inv_freq[None, :]
    cos, sin = jnp.cos(ang), jnp.sin(ang)
    reps = LANES // HEAD_DIM
    return jnp.tile(cos, (1, 2 * reps)), jnp.tile(jnp.concatenate([-sin, sin], axis=1), (1, reps))


def _in_proj_kernel(x_ref, w_ref, cos_ref, sin_ref, q_ref, k_ref, v_ref, g_ref):
    xb = x_ref[...].astype(BF16)
    cos = cos_ref[...]
    sin = sin_ref[...]
    lane = lax.broadcasted_iota(jnp.int32, cos.shape, 1)
    first_half = (lane % HEAD_DIM) < (HEAD_DIM // 2)

    def rope(acc):
        outs = []
        for c in range(acc.shape[1] // LANES):
            a = acc[:, c * LANES:(c + 1) * LANES]
            partner = jnp.where(first_half, pltpu.roll(a, LANES - HEAD_DIM // 2, 1),
                                pltpu.roll(a, HEAD_DIM // 2, 1))
            outs.append(a * cos + partner * sin)
        return jnp.concatenate(outs, axis=1)

    o_k = ATTN_WIDTH
    o_v = o_k + KV_WIDTH
    o_a = o_v + KV_WIDTH
    ch = (w_ref.shape[1] - o_a) // 2
    q = rope(_dot(xb, w_ref[:, 0:o_k]))
    q_ref[...] = (q * (HEAD_DIM ** -0.5)).astype(BF16)
    k_ref[...] = rope(_dot(xb, w_ref[:, o_k:o_v]))
    v_ref[...] = _dot(xb, w_ref[:, o_v:o_a])
    a = _dot(xb, w_ref[:, o_a:o_a + ch])
    gate = _dot(xb, w_ref[:, o_a + ch:o_a + 2 * ch])
    g_ref[...] = a * _sigmoid(gate)


def _in_proj(x2d, w_bf16, cos, sin, tm):
    t, d = x2d.shape
    n = w_bf16.shape[1]
    ch = (n - ATTN_WIDTH - 2 * KV_WIDTH) // 2
    ntab = cos.shape[0] // tm
    return pl.pallas_call(
        _in_proj_kernel,
        grid=(t // tm,),
        in_specs=[
            pl.BlockSpec((tm, d), lambda i: (i, 0)),
            pl.BlockSpec((d, n), lambda i: (0, 0), pipeline_mode=pl.Buffered(1)),
            pl.BlockSpec((tm, LANES), lambda i: (i % ntab, 0)),
            pl.BlockSpec((tm, LANES), lambda i: (i % ntab, 0)),
        ],
        out_specs=[
            pl.BlockSpec((tm, ATTN_WIDTH), lambda i: (i, 0)),
            pl.BlockSpec((tm, KV_WIDTH), lambda i: (i, 0)),
            pl.BlockSpec((tm, KV_WIDTH), lambda i: (i, 0)),
            pl.BlockSpec((tm, ch), lambda i: (i, 0)),
        ],
        out_shape=[
            jax.ShapeDtypeStruct((t, ATTN_WIDTH), BF16),
            jax.ShapeDtypeStruct((t, KV_WIDTH), F32),
            jax.ShapeDtypeStruct((t, KV_WIDTH), F32),
            jax.ShapeDtypeStruct((t, ch), F32),
        ],
        compiler_params=_cparams("parallel"),
        name="in_proj",
    )(x2d, w_bf16, cos, sin)


def _sink_softmax_pv(s, mask, sink, v_bf16):
    s = jnp.where(mask, s, -jnp.inf)
    m = jnp.maximum(jnp.max(s, axis=-1, keepdims=True), sink)
    e = jnp.exp(s - m)
    den = jnp.sum(e, axis=-1, keepdims=True) + jnp.exp(sink - m)
    return _dot(e.astype(BF16), v_bf16) / den


def _prompt_attn_kernel(sink_ref, q_ref, kc_ref, kp_ref, vc_ref, vp_ref, o_ref):
    j = pl.program_id(1)
    q = q_ref[...]
    kb = jnp.concatenate([kp_ref[...], kc_ref[...]], axis=0).astype(BF16)
    vb = jnp.concatenate([vp_ref[...], vc_ref[...]], axis=0).astype(BF16)
    r = lax.broadcasted_iota(jnp.int32, (WINDOW, 2 * WINDOW), 0)
    s_idx = lax.broadcasted_iota(jnp.int32, (WINDOW, 2 * WINDOW), 1)
    mask = (s_idx > r) & (s_idx <= r + WINDOW) & ((s_idx >= WINDOW) | (j > 0))
    outs = []
    for kv in range(N_KV_HEADS):
        kk = kb[:, kv * HEAD_DIM:(kv + 1) * HEAD_DIM]
        vv = vb[:, kv * HEAD_DIM:(kv + 1) * HEAD_DIM]
        for g in range(Q_PER_KV):
            h = kv * Q_PER_KV + g
            qh = q[:, h * HEAD_DIM:(h + 1) * HEAD_DIM]
            s = lax.dot_general(qh, kk, (((1,), (1,)), ((), ())), preferred_element_type=F32)
            outs.append(_sink_softmax_pv(s, mask, sink_ref[0, h], vv))
    o_ref[...] = jnp.concatenate(outs, axis=1).astype(o_ref.dtype)


def _prompt_attn(q, k, v, sinks, batch, seq):
    nb = seq // WINDOW
    cur = lambda b, j: (b * nb + j, 0)
    prev = lambda b, j: (b * nb + jnp.maximum(j - 1, 0), 0)
    return pl.pallas_call(
        _prompt_attn_kernel,
        grid=(batch, nb),
        in_specs=[
            pl.BlockSpec(memory_space=pltpu.SMEM),
            pl.BlockSpec((WINDOW, ATTN_WIDTH), cur),
            pl.BlockSpec((WINDOW, KV_WIDTH), cur),
            pl.BlockSpec((WINDOW, KV_WIDTH), prev),
            pl.BlockSpec((WINDOW, KV_WIDTH), cur),
            pl.BlockSpec((WINDOW, KV_WIDTH), prev),
        ],
        out_specs=pl.BlockSpec((WINDOW, ATTN_WIDTH), cur),
        out_shape=jax.ShapeDtypeStruct((batch * seq, ATTN_WIDTH), BF16),
        compiler_params=_cparams("parallel", "arbitrary"),
        name="prompt_attn",
    )(sinks, q, k, k, v, v)


SAMPLE_BB = 8
NEW_PAD = 8


def _sample_attn_kernel(sink_ref, q_ref, kn_ref, vn_ref, kc_ref, vc_ref, o_ref, *, dec_seq):
    lc = kc_ref.shape[1]
    rows = dec_seq * Q_PER_KV
    per_pad = NEW_PAD // dec_seq
    t_q = lax.broadcasted_iota(jnp.int32, (rows, lc + NEW_PAD), 0) // Q_PER_KV
    col = lax.broadcasted_iota(jnp.int32, (rows, lc + NEW_PAD), 1)
    g_idx = lax.broadcasted_iota(jnp.int32, (rows, 1), 0) % Q_PER_KV
    for bi in range(SAMPLE_BB):
        lo = (bi % per_pad) * dec_seq
        base = (bi // per_pad) * NEW_PAD
        kall = jnp.concatenate([kc_ref[bi], kn_ref[base:base + NEW_PAD, :]], axis=0).astype(BF16)
        vall = jnp.concatenate([vc_ref[bi], vn_ref[base:base + NEW_PAD, :]], axis=0).astype(BF16)
        new_t = col - lc - lo
        mask = ((col < lc) & (col + WINDOW > t_q + lc)) | ((new_t >= 0) & (new_t <= t_q))
        for kv in range(N_KV_HEADS):
            kk = kall[:, kv * HEAD_DIM:(kv + 1) * HEAD_DIM]
            vv = vall[:, kv * HEAD_DIM:(kv + 1) * HEAD_DIM]
            sink = jnp.zeros((rows, 1), F32)
            for g in range(Q_PER_KV):
                sink = jnp.where(g_idx == g, sink_ref[0, kv * Q_PER_KV + g], sink)
            s = lax.dot_general(q_ref[bi, kv], kk, (((1,), (1,)), ((), ())), preferred_element_type=F32)
            o_ref[bi, kv] = _sink_softmax_pv(s, mask, sink, vv)


def _sample_attn(q, k, v, sinks, cache_k, cache_v, dec_batch, dec_seq):
    lc = cache_k.shape[1]
    rows = dec_seq * Q_PER_KV
    q5 = q.reshape(dec_batch, dec_seq, N_KV_HEADS, Q_PER_KV, HEAD_DIM).transpose(0, 2, 1, 3, 4)
    q4 = q5.reshape(dec_batch, N_KV_HEADS, rows, HEAD_DIM)
    blk_q = pl.BlockSpec((SAMPLE_BB, N_KV_HEADS, rows, HEAD_DIM), lambda i: (i, 0, 0, 0))
    blk_new = pl.BlockSpec((SAMPLE_BB * dec_seq, KV_WIDTH), lambda i: (i, 0))
    blk_cache = pl.BlockSpec((SAMPLE_BB, lc, KV_WIDTH), lambda i: (i, 0, 0))
    o = pl.pallas_call(
        functools.partial(_sample_attn_kernel, dec_seq=dec_seq),
        grid=(dec_batch // SAMPLE_BB,),
        in_specs=[pl.BlockSpec(memory_space=pltpu.SMEM), blk_q, blk_new, blk_new, blk_cache, blk_cache],
        out_specs=blk_q,
        out_shape=jax.ShapeDtypeStruct(q4.shape, F32),
        compiler_params=_cparams("parallel"),
        name="sample_attn",
    )(sinks, q4, k, v, cache_k, cache_v)
    o = o.reshape(dec_batch, N_KV_HEADS, dec_seq, Q_PER_KV, HEAD_DIM).transpose(0, 2, 1, 3, 4)
    return o.reshape(dec_batch * dec_seq, ATTN_WIDTH).astype(BF16)


CONV_ROWS = 32
CONV_ROWS_GROUPED = 64


def _conv_taps(ext_ref, w_ref, y_ref, n_out, first_row):
    ch = y_ref.shape[-1]
    for r0 in range(0, n_out, CONV_ROWS):
        nr = min(CONV_ROWS, n_out - r0)
        for c0 in range(0, ch, LANES):
            acc = jnp.zeros((nr, LANES), F32)
            for j in range(CONV_WIDTH):
                acc = acc + ext_ref[pl.ds(first_row + r0 + j, nr), c0:c0 + LANES] * w_ref[j:j + 1, c0:c0 + LANES]
            y_ref[r0:r0 + nr, c0:c0 + LANES] = acc


def _conv_taps_grouped(ext_ref, w_ref, y_ref, n_out, first_row):
    ch = y_ref.shape[-1]
    sub = 8
    groups = [[j for j in range(CONV_WIDTH) if (first_row + j) % sub == res] for res in range(sub)]
    for r0 in range(0, n_out, CONV_ROWS_GROUPED):
        nr = min(CONV_ROWS_GROUPED, n_out - r0)
        for c0 in range(0, ch, LANES):
            y = jnp.zeros((nr, LANES), F32)
            for res, taps in enumerate(groups):
                rows = nr if res == 0 else nr + sub
                acc = jnp.zeros((rows, LANES), F32)
                for j in taps:
                    acc = acc + ext_ref[pl.ds(r0 + first_row + j - res, rows), c0:c0 + LANES] * w_ref[j:j + 1, c0:c0 + LANES]
                if taps:
                    y = y + acc[res:res + nr, :]
            y_ref[r0:r0 + nr, c0:c0 + LANES] = y


def _conv_post(y, b_ref, g_ref, beta_ref):
    z = _layer_norm(y + b_ref[...], g_ref[...], beta_ref[...])
    return z * _sigmoid(z)


def _conv_prompt_kernel(cur_ref, halo_ref, w_ref, b_ref, g_ref, beta_ref, o_ref, ext_ref, y_ref):
    i = pl.program_id(1)
    tt = cur_ref.shape[0]
    ext_ref[0:CONV_HALO, :] = jnp.where(i > 0, halo_ref[...], 0.0)
    ext_ref[CONV_HALO:CONV_HALO + tt, :] = cur_ref[...]
    _conv_taps_grouped(ext_ref, w_ref, y_ref, tt, CONV_HALO - (CONV_WIDTH - 1))
    o_ref[...] = _conv_post(y_ref[...], b_ref, g_ref, beta_ref).astype(o_ref.dtype)


def _conv_prompt(g, w, b, ln_g, ln_b, batch, seq, tt):
    ch = g.shape[1]
    nt = seq // tt
    hb = tt // CONV_HALO
    vec = pl.BlockSpec((1, ch), lambda bi, i: (0, 0))
    return pl.pallas_call(
        _conv_prompt_kernel,
        grid=(batch, nt),
        in_specs=[
            pl.BlockSpec((tt, ch), lambda bi, i: (bi * nt + i, 0)),
            pl.BlockSpec((CONV_HALO, ch), lambda bi, i: (jnp.maximum((bi * nt + i) * hb - 1, 0), 0)),
            pl.BlockSpec((CONV_WIDTH, ch), lambda bi, i: (0, 0)),
            vec, vec, vec,
        ],
        out_specs=pl.BlockSpec((tt, ch), lambda bi, i: (bi * nt + i, 0)),
        out_shape=jax.ShapeDtypeStruct(g.shape, BF16),
        scratch_shapes=[pltpu.VMEM((CONV_HALO + tt, ch), F32), pltpu.VMEM((tt, ch), F32)],
        compiler_params=_cparams("parallel", "arbitrary"),
        name="conv_prompt",
    )(g, g, w, b, ln_g, ln_b)


def _conv_sample_kernel(ext_ref, w_ref, b_ref, g_ref, beta_ref, o_ref, y_ref):
    nb, _, ch = ext_ref.shape
    dec = o_ref.shape[1]
    for bi in range(nb):
        _conv_taps(ext_ref.at[bi], w_ref, y_ref.at[bi], dec, 0)
    for bi in range(nb):
        o_ref[bi] = _conv_post(y_ref[bi], b_ref, g_ref, beta_ref)


def _conv_sample(ext, w, b, ln_g, ln_b, dec_seq):
    nb, rows, ch = ext.shape
    vec = pl.BlockSpec((1, ch), lambda i: (0, 0))
    return pl.pallas_call(
        _conv_sample_kernel,
        grid=(nb // SAMPLE_BB,),
        in_specs=[
            pl.BlockSpec((SAMPLE_BB, rows, ch), lambda i: (i, 0, 0)),
            pl.BlockSpec((CONV_WIDTH, ch), lambda i: (0, 0)),
            vec, vec, vec,
        ],
        out_specs=pl.BlockSpec((SAMPLE_BB, dec_seq, ch), lambda i: (i, 0, 0)),
        out_shape=jax.ShapeDtypeStruct((nb, dec_seq, ch), F32),
        scratch_shapes=[pltpu.VMEM((SAMPLE_BB, dec_seq, ch), F32)],
        compiler_params=_cparams("parallel"),
        name="conv_sample",
    )(ext, w, b, ln_g, ln_b)


def _bf16_words(vals):
    half = vals.shape[1] // 2
    bits = lax.bitcast_convert_type(vals, jnp.uint32)
    return (bits[:, :half] >> 16) | (bits[:, half:] & jnp.uint32(0xFFFF0000))


def _word_halves(words):
    lo = lax.bitcast_convert_type(words << 16, F32)
    hi = lax.bitcast_convert_type(words & jnp.uint32(0xFFFF0000), F32)
    return lo, hi


def _store_packed(ref, words, base=0):
    n = words.shape[0]
    for s in range(PACK_ROWS):
        ref[pl.ds(base + s, n, stride=PACK_ROWS), :] = words[:, s * LANES:(s + 1) * LANES]


def _mix_kernel(ap_ref, as_ref, cp_ref, cs_ref, xp_ref, xs_ref, wo_ref, ag_ref, cg_ref, g1_ref, b1_ref, rwt_ref,
                sg_ref, su_ref, sd_ref, base_ref, hbp_ref, sct_ref, *, n_prompt_tiles):
    is_p = pl.program_id(0) < n_prompt_tiles
    attn = jnp.where(is_p, ap_ref[...], as_ref[...]).astype(F32)
    conv = jnp.where(is_p, cp_ref[...], cs_ref[...]).astype(F32)
    x = jnp.where(is_p, xp_ref[...], xs_ref[...])
    aw = attn.shape[1]
    a = _rms_norm(attn, ag_ref[...]).astype(BF16)
    c = _rms_norm(conv, cg_ref[...]).astype(BF16)
    mixed = _dot(a, wo_ref[0:aw, :]) + _dot(c, wo_ref[aw:, :])
    h = _layer_norm(DEEPNORM_ALPHA * x + mixed, g1_ref[...], b1_ref[...])
    hb = h.astype(BF16)
    _store_packed(hbp_ref, _bf16_words(hb.astype(F32)))
    logits_t = lax.dot_general(rwt_ref[...], hb, (((1,), (1,)), ((), ())), preferred_element_type=F32)
    sct_ref[...] = _sigmoid(logits_t)
    h1 = _dot(hb, sg_ref[...])
    act = (h1 * _sigmoid(h1) * _dot(hb, su_ref[...])).astype(BF16)
    base_ref[...] = DEEPNORM_ALPHA * h + _dot(act, sd_ref[...])


def _mix(attn_p, attn_s, conv_p, conv_s, xp, xs, wo, ag, cg, g1, b1, rwt, sg, su, sd, tm):
    tp, d = xp.shape
    ts = xs.shape[0]
    n_p, n_s = tp // tm, ts // tm
    ne = rwt.shape[0]
    t = tp + ts
    prow = lambda w: pl.BlockSpec((tm, w), lambda i: (jnp.minimum(i, n_p - 1), 0))
    srow = lambda w: pl.BlockSpec((tm, w), lambda i: (jnp.maximum(i - n_p, 0), 0))
    full = lambda a: pl.BlockSpec(a.shape, lambda i: (0,) * a.ndim, pipeline_mode=pl.Buffered(1))
    aw, cw = attn_p.shape[1], conv_p.shape[1]
    return pl.pallas_call(
        functools.partial(_mix_kernel, n_prompt_tiles=n_p),
        grid=(n_p + n_s,),
        in_specs=[prow(aw), srow(aw), prow(cw), srow(cw), prow(d), srow(d)]
        + [full(a) for a in (wo, ag, cg, g1, b1, rwt, sg, su, sd)],
        out_specs=[pl.BlockSpec((tm, d), lambda i: (i, 0)),
                   pl.BlockSpec((tm * PACK_ROWS, LANES), lambda i: (i, 0)),
                   pl.BlockSpec((ne, tm), lambda i: (0, i))],
        out_shape=[jax.ShapeDtypeStruct((t, d), F32),
                   jax.ShapeDtypeStruct((t * PACK_ROWS, LANES), jnp.uint32),
                   jax.ShapeDtypeStruct((ne, t), F32)],
        compiler_params=_cparams("arbitrary"),
        name="mix_ln1",
    )(attn_p, attn_s, conv_p, conv_s, xp, xs, wo, ag, cg, g1, b1, rwt, sg, su, sd)


def _route_kernel(sc_ref, bias_ref, tri_ref, eidx_ref, gate_ref, rank_ref, cnt_ref, carry_ref, mask_ref, eidf_ref):
    ne, tn = sc_ref.shape
    per = ne // N_EXPERT_GROUPS
    ninf = -jnp.inf

    @pl.when(pl.program_id(0) == 0)
    def _():
        carry_ref[...] = jnp.zeros_like(carry_ref)

    erow = lax.broadcasted_iota(jnp.int32, (ne, LANES), 0).astype(F32)
    grow = lax.broadcasted_iota(jnp.int32, (per, LANES), 0).astype(F32)
    gi = lax.broadcasted_iota(jnp.int32, (N_EXPERT_GROUPS, LANES), 0).astype(F32)
    colmax = lambda v: jnp.max(v, axis=0, keepdims=True)
    first_at = lambda v, m, idx, n: jnp.min(jnp.where(v == m, idx, float(n)), axis=0, keepdims=True)

    for c in range(tn // LANES):
        lanes = slice(c * LANES, (c + 1) * LANES)
        sc = sc_ref[:, lanes]
        sel = sc + bias_ref[...]
        gs = []
        for g in range(N_EXPERT_GROUPS):
            v = sel[g * per:(g + 1) * per, :]
            m1 = colmax(v)
            m2 = colmax(jnp.where(grow == first_at(v, m1, grow, per), ninf, v))
            gs.append(m1 + m2)
        grp = jnp.concatenate(gs, axis=0)
        gsel = jnp.zeros_like(grp)
        for _ in range(TOPK_GROUPS):
            m = colmax(grp)
            hit = gi == first_at(grp, m, gi, N_EXPERT_GROUPS)
            gsel = jnp.where(hit, 1.0, gsel)
            grp = jnp.where(hit, ninf, grp)
        emask = jnp.concatenate([jnp.broadcast_to(gsel[g:g + 1, :], (per, LANES)) for g in range(N_EXPERT_GROUPS)], axis=0)
        masked = jnp.where(emask > 0.5, sel, ninf)
        selm = jnp.zeros_like(sel)
        idxs, gts = [], []
        for _ in range(TOP_K):
            ix = first_at(masked, colmax(masked), erow, ne)
            hit = erow == ix
            gts.append(jnp.sum(jnp.where(hit, sc, 0.0), axis=0, keepdims=True))
            idxs.append(ix)
            masked = jnp.where(hit, ninf, masked)
            selm = jnp.where(hit, 1.0, selm)
        gsum = gts[0]
        for g in gts[1:]:
            gsum = gsum + g
        eidf = jnp.concatenate(idxs, axis=0)
        eidf_ref[:, lanes] = eidf
        eidx_ref[:, lanes] = eidf.astype(jnp.int32)
        gate_ref[:, lanes] = jnp.concatenate([g / gsum * ROUTED_SCALE for g in gts], axis=0)
        mask_ref[:, lanes] = selm

    selm_all = mask_ref[...]
    incl = _dot(selm_all.astype(BF16), tri_ref[...])
    carry = carry_ref[...]
    for c in range(tn // LANES):
        lanes = slice(c * LANES, (c + 1) * LANES)
        before = incl[:, lanes] - selm_all[:, lanes] + carry
        eidf = eidf_ref[:, lanes]
        rk = [jnp.sum(jnp.where(erow == eidf[k:k + 1, :], before, 0.0), axis=0, keepdims=True) for k in range(TOP_K)]
        rank_ref[:, lanes] = jnp.concatenate(rk, axis=0).astype(jnp.int32)
    total = carry + jnp.broadcast_to(incl[:, tn - 1:tn], (ne, LANES))
    carry_ref[...] = total
    cnt_ref[...] = total.astype(jnp.int32)


def _route(sct, bias_l, tn):
    ne, t = sct.shape
    tri = jnp.triu(jnp.ones((tn, tn), BF16))
    col = pl.BlockSpec((TOP_K, tn), lambda i: (0, i))
    return pl.pallas_call(
        _route_kernel,
        grid=(t // tn,),
        in_specs=[pl.BlockSpec((ne, tn), lambda i: (0, i)),
                  pl.BlockSpec((ne, LANES), lambda i: (0, 0)),
                  pl.BlockSpec((tn, tn), lambda i: (0, 0))],
        out_specs=[col, col, col, pl.BlockSpec((ne, LANES), lambda i: (0, 0))],
        out_shape=[jax.ShapeDtypeStruct((TOP_K, t), jnp.int32), jax.ShapeDtypeStruct((TOP_K, t), F32),
                   jax.ShapeDtypeStruct((TOP_K, t), jnp.int32), jax.ShapeDtypeStruct((ne, LANES), jnp.int32)],
        scratch_shapes=[pltpu.VMEM((ne, LANES), F32), pltpu.VMEM((ne, tn), F32), pltpu.VMEM((TOP_K, tn), F32)],
        compiler_params=_cparams("arbitrary"),
        name="route",
    )(sct, bias_l, tri)


def _pos_kernel(pstart_ref, eidx_ref, rank_ref, pos_ref):
    eid = eidx_ref[...]
    base = lax.fori_loop(0, pstart_ref.shape[0], lambda e, acc: jnp.where(eid == e, pstart_ref[e], acc),
                         jnp.zeros_like(eid), unroll=DMA_UNROLL)
    pos_ref[...] = base + rank_ref[...]


def _positions(pstart, eidx_t, rank_t, tm):
    t = eidx_t.shape[1]
    col = pl.BlockSpec((TOP_K, tm), lambda i, ps: (0, i))
    return pl.pallas_call(
        _pos_kernel,
        grid_spec=pltpu.PrefetchScalarGridSpec(
            num_scalar_prefetch=1, grid=(t // tm,), in_specs=[col, col],
            out_specs=pl.BlockSpec((None, TOP_K, tm), lambda i, ps: (i, 0, 0))),
        out_shape=jax.ShapeDtypeStruct((t // tm, TOP_K, tm), jnp.int32),
        compiler_params=_cparams("parallel"),
        name="positions",
    )(pstart, eidx_t, rank_t)


def _packed_rows(ref, row):
    return ref.at[pl.ds(pl.multiple_of(row * PACK_ROWS, PACK_ROWS), PACK_ROWS), :]


def _dispatch_kernel(last_ref, pos_hbm, hb_ref, xs_hbm, pos_s, zero_ref, sem_pos, sem_zero, sem_row):
    i = pl.program_id(0)
    tm = pos_s.shape[1]
    tile_rows = zero_ref.shape[0]

    @pl.when(i == 0)
    def _():
        zero_ref[...] = jnp.zeros_like(zero_ref)

        def fill(e):
            start = pl.multiple_of(last_ref[e] * PACK_ROWS, PACK_ROWS)
            return pltpu.make_async_copy(zero_ref, xs_hbm.at[pl.ds(start, tile_rows), :], sem_zero)

        def go(e, c):
            @pl.when(last_ref[e] >= 0)
            def _():
                fill(e).start()
            return c

        def done(e, c):
            @pl.when(last_ref[e] >= 0)
            def _():
                fill(e).wait()
            return c

        def tail(t):
            start = pl.multiple_of(t * tile_rows, PACK_ROWS)
            return pltpu.make_async_copy(zero_ref, xs_hbm.at[pl.ds(start, tile_rows), :], sem_zero)

        def tail_go(t, c):
            tail(t).start()
            return c

        def tail_done(t, c):
            tail(t).wait()
            return c

        n_experts = last_ref.shape[0] - 1
        n_active, n_tiles = last_ref[n_experts], xs_hbm.shape[0] // tile_rows
        lax.fori_loop(0, n_experts, go, 0)
        lax.fori_loop(n_active, n_tiles, tail_go, 0)
        lax.fori_loop(0, n_experts, done, 0)
        lax.fori_loop(n_active, n_tiles, tail_done, 0)

    get_pos = pltpu.make_async_copy(pos_hbm.at[i], pos_s, sem_pos)
    get_pos.start()
    get_pos.wait()

    def row_copy(r, k):
        return pltpu.make_async_copy(_packed_rows(hb_ref, r), _packed_rows(xs_hbm, pos_s[k, r]), sem_row)

    def issue(r, c):
        for k in range(TOP_K):
            row_copy(r, k).start(priority=k % 2)
        return c

    def drain(r, c):
        for k in range(TOP_K):
            row_copy(r, k).wait()
        return c

    lax.fori_loop(0, tm, issue, 0, unroll=DMA_UNROLL)
    lax.fori_loop(0, tm, drain, 0, unroll=DMA_UNROLL)


def _dispatch(fill_plan, pos3, hbp, n_rows, tm, expert_tile):
    n_tok_tiles = pos3.shape[0]
    return pl.pallas_call(
        _dispatch_kernel,
        grid_spec=pltpu.PrefetchScalarGridSpec(
            num_scalar_prefetch=1, grid=(n_tok_tiles,),
            in_specs=[pl.BlockSpec(memory_space=pl.ANY),
                      pl.BlockSpec((tm * PACK_ROWS, LANES), lambda i, lt: (i, 0))],
            out_specs=pl.BlockSpec(memory_space=pl.ANY),
            scratch_shapes=[pltpu.SMEM((TOP_K, tm), jnp.int32),
                            pltpu.VMEM((expert_tile * PACK_ROWS, LANES), jnp.uint32),
                            pltpu.SemaphoreType.DMA(()), pltpu.SemaphoreType.DMA(()), pltpu.SemaphoreType.DMA(())]),
        out_shape=jax.ShapeDtypeStruct((n_rows * PACK_ROWS, LANES), jnp.uint32),
        compiler_params=_cparams("arbitrary"),
        name="dispatch",
    )(fill_plan, pos3, hbp)


def _moe_kernel(first_ref, count_ref, x_hbm, wg_ref, wu_ref, wd_ref, y_hbm, xbuf, ybuf, xrem, yrem, wgb, wub, wdb,
                pend, sem_x, sem_y, sem_r):
    e = pl.program_id(0)
    n, t0 = count_ref[e], first_ref[e]
    tile_rows = xrem.shape[0]
    pair_rows = 2 * tile_rows
    n_pairs = n // 2
    odd = n - 2 * n_pairs

    def hbm_rows(ref, tile, rows):
        return ref.at[pl.ds(pl.multiple_of(tile * tile_rows, PACK_ROWS), rows), :]

    def slot_rows(ref, slot):
        return ref.at[pl.ds(pl.multiple_of(slot * pair_rows, PACK_ROWS), pair_rows), :]

    def x_copy(j, slot):
        return pltpu.make_async_copy(hbm_rows(x_hbm, t0 + 2 * j, pair_rows), slot_rows(xbuf, slot), sem_x.at[slot])

    def y_copy(first, j, slot):
        return pltpu.make_async_copy(slot_rows(ybuf, slot), hbm_rows(y_hbm, first + 2 * j, pair_rows), sem_y.at[slot])

    def y_last(first, pairs):
        return pltpu.make_async_copy(yrem, hbm_rows(y_hbm, first + 2 * pairs, tile_rows), sem_r.at[1])

    x_last = pltpu.make_async_copy(hbm_rows(x_hbm, t0 + 2 * n_pairs, tile_rows), xrem, sem_r.at[0])

    def drain_previous():
        first, count = pend[1], pend[2]
        pairs = count // 2

        @pl.when(pairs >= 2)
        def _():
            y_copy(first, pairs - 2, pairs % 2).wait()

        @pl.when(pairs >= 1)
        def _():
            y_copy(first, pairs - 1, (pairs - 1) % 2).wait()

        @pl.when(count - 2 * pairs == 1)
        def _():
            y_last(first, pairs).wait()

        pend[0] = 0

    def experts_mlp(src, src_base, dst, dst_base, rows):
        tm = rows // PACK_ROWS
        halves = [_word_halves(src[pl.ds(src_base + s, tm, stride=PACK_ROWS), :]) for s in range(PACK_ROWS)]
        x = jnp.concatenate([h[0] for h in halves] + [h[1] for h in halves], axis=1).astype(BF16)
        h1 = _dot(x, wgb[...])
        act = (h1 * _sigmoid(h1) * _dot(x, wub[...])).astype(BF16)
        y = _dot(act, wdb[...]).astype(BF16)
        _store_packed(dst, _bf16_words(y.astype(F32)), dst_base)

    @pl.when(e == 0)
    def _():
        pend[0] = 0

    @pl.when(n > 0)
    def _():
        @pl.when(n_pairs > 0)
        def _():
            x_copy(0, 0).start()

        @pl.when(odd == 1)
        def _():
            x_last.start()

        wgb[...] = wg_ref[...].astype(BF16)
        wub[...] = wu_ref[...].astype(BF16)
        wdb[...] = wd_ref[...].astype(BF16)

        @pl.when(pend[0] == 1)
        def _():
            drain_previous()

        @pl.when(odd == 1)
        def _():
            x_last.wait()
            experts_mlp(xrem, 0, yrem, 0, tile_rows)
            y_last(t0, n_pairs).start()

        def step(j, c):
            slot = j % 2
            x_copy(j, slot).wait()

            @pl.when(j + 1 < n_pairs)
            def _():
                x_copy(j + 1, 1 - slot).start()

            @pl.when(j >= 2)
            def _():
                y_copy(t0, j - 2, slot).wait()

            experts_mlp(xbuf, slot * pair_rows, ybuf, slot * pair_rows, pair_rows)
            y_copy(t0, j, slot).start()
            return c

        lax.fori_loop(0, n_pairs, step, 0)
        pend[0] = 1
        pend[1] = t0
        pend[2] = n

    @pl.when((e == pl.num_programs(0) - 1) & (pend[0] == 1))
    def _():
        drain_previous()


def _moe(xs, w_gate, w_up, w_down, tile_start, tiles_e, tm):
    ne, d, ff = w_gate.shape
    tile_rows = tm * PACK_ROWS
    wmap = lambda e, ts, nt: (e, 0, 0)
    return pl.pallas_call(
        _moe_kernel,
        grid_spec=pltpu.PrefetchScalarGridSpec(
            num_scalar_prefetch=2,
            grid=(ne,),
            in_specs=[
                pl.BlockSpec(memory_space=pl.ANY),
                pl.BlockSpec((None, d, ff), wmap),
                pl.BlockSpec((None, d, ff), wmap),
                pl.BlockSpec((None, ff, d), wmap),
            ],
            out_specs=pl.BlockSpec(memory_space=pl.ANY),
            scratch_shapes=[pltpu.VMEM((4 * tile_rows, LANES), jnp.uint32), pltpu.VMEM((4 * tile_rows, LANES), jnp.uint32),
                            pltpu.VMEM((tile_rows, LANES), jnp.uint32), pltpu.VMEM((tile_rows, LANES), jnp.uint32),
                            pltpu.VMEM((d, ff), BF16), pltpu.VMEM((d, ff), BF16), pltpu.VMEM((ff, d), BF16),
                            pltpu.SMEM((3,), jnp.int32),
                            pltpu.SemaphoreType.DMA((2,)), pltpu.SemaphoreType.DMA((2,)), pltpu.SemaphoreType.DMA((2,))],
        ),
        out_shape=jax.ShapeDtypeStruct(xs.shape, jnp.uint32),
        input_output_aliases={2: 0},
        compiler_params=_cparams("arbitrary"),
        name="moe_experts",
    )(tile_start, tiles_e, xs, w_gate, w_up, w_down)


COMBINE_ROWS = 64


def _combine_kernel(pos_hbm, y_hbm, base_ref, gate_ref, g_ref, b_ref, op_ref, os_ref,
                    pos_s, ybuf, gb_ref, f_ref, sem_pos, sem_row, *, n_prompt_tiles):
    i = pl.program_id(0)
    tm, d = base_ref.shape
    slot = i % 2
    slot_tokens = TOP_K * tm

    def row_copy(s, r, k):
        return pltpu.make_async_copy(_packed_rows(y_hbm, pos_s[s, k, r]),
                                     _packed_rows(ybuf, s * slot_tokens + k * tm + r), sem_row.at[s])

    def start_gather(tile, s):
        get_pos = pltpu.make_async_copy(pos_hbm.at[tile], pos_s.at[s], sem_pos)
        get_pos.start()
        get_pos.wait()

        def issue(r, c):
            for k in range(TOP_K):
                row_copy(s, r, k).start(priority=k % 2)
            return c

        lax.fori_loop(0, tm, issue, 0, unroll=DMA_UNROLL)

    @pl.when(i == 0)
    def _():
        start_gather(0, 0)

    @pl.when(i + 1 < pl.num_programs(0))
    def _():
        start_gather(i + 1, 1 - slot)

    gates = gate_ref[...]
    for k in range(TOP_K):
        gb_ref[k] = jnp.broadcast_to(gates[:, k:k + 1], (tm, LANES))

    def drain(r, c):
        for k in range(TOP_K):
            row_copy(slot, r, k).wait()
        return c

    lax.fori_loop(0, tm, drain, 0, unroll=DMA_UNROLL)

    for r0 in range(0, tm, COMBINE_ROWS):
        nr = min(COMBINE_ROWS, tm - r0)
        for s in range(PACK_ROWS):
            c_lo, c_hi = s * LANES, (PACK_ROWS + s) * LANES
            lo = base_ref[r0:r0 + nr, c_lo:c_lo + LANES]
            hi = base_ref[r0:r0 + nr, c_hi:c_hi + LANES]
            for k in range(TOP_K):
                first = (slot * slot_tokens + k * tm + r0) * PACK_ROWS + s
                w_lo, w_hi = _word_halves(ybuf[pl.ds(first, nr, stride=PACK_ROWS), :])
                g = gb_ref[k, r0:r0 + nr, :]
                lo = lo + g * w_lo
                hi = hi + g * w_hi
            f_ref[r0:r0 + nr, c_lo:c_lo + LANES] = lo
            f_ref[r0:r0 + nr, c_hi:c_hi + LANES] = hi
    y = _layer_norm(f_ref[...], g_ref[...], b_ref[...])

    @pl.when(i < n_prompt_tiles)
    def _():
        op_ref[...] = y

    @pl.when(i >= n_prompt_tiles)
    def _():
        os_ref[...] = y


def _combine(pos3, y_sorted, base, gates, ln_g, ln_b, tp, tm):
    t, d = base.shape
    n_p = tp // tm
    vec = pl.BlockSpec((1, d), lambda i: (0, 0))
    return pl.pallas_call(
        functools.partial(_combine_kernel, n_prompt_tiles=n_p),
        grid=(t // tm,),
        in_specs=[pl.BlockSpec(memory_space=pl.ANY), pl.BlockSpec(memory_space=pl.ANY),
                  pl.BlockSpec((tm, d), lambda i: (i, 0)), pl.BlockSpec((tm, TOP_K), lambda i: (i, 0)), vec, vec],
        out_specs=[pl.BlockSpec((tm, d), lambda i: (jnp.minimum(i, n_p - 1), 0)),
                   pl.BlockSpec((tm, d), lambda i: (jnp.maximum(i - n_p, 0), 0))],
        out_shape=[jax.ShapeDtypeStruct((tp, d), F32), jax.ShapeDtypeStruct((t - tp, d), F32)],
        scratch_shapes=[pltpu.SMEM((2, TOP_K, tm), jnp.int32),
                        pltpu.VMEM((2 * TOP_K * tm * PACK_ROWS, LANES), jnp.uint32),
                        pltpu.VMEM((TOP_K, tm, LANES), F32),
                        pltpu.VMEM((tm, d), F32),
                        pltpu.SemaphoreType.DMA(()), pltpu.SemaphoreType.DMA((2,))],
        compiler_params=_cparams("arbitrary"),
        name="combine_ln2",
    )(pos3, y_sorted, base, gates, ln_g, ln_b)


def kernel(x_prompt, x_sample, cache_win_k, cache_win_v, state_conv, w_in, attn_sinks, conv_w, conv_b, conv_ln_g, conv_ln_b, attn_out_g, conv_out_g, w_out, ln1_g, ln1_b, router_w, router_b, exp_w_gate, exp_w_up, exp_w_down, sh_w_gate, sh_w_up, sh_w_down, ln2_g, ln2_b):
    assert w_in.shape[0] == DEPTH
    batch, seq, d = x_prompt.shape
    dec_batch, dec_seq, _ = x_sample.shape
    lc = cache_win_k.shape[2]
    ch = conv_w.shape[2]
    ne = router_w.shape[2]
    tp, ts = batch * seq, dec_batch * dec_seq
    t_all = tp + ts
    assert d == 2 * PACK_ROWS * LANES
    row2 = lambda a: a.reshape(1, a.shape[-1])

    w_in_b = w_in.reshape(w_in.shape[1:]).astype(BF16)
    w_out_b = w_out.reshape(w_out.shape[1:]).astype(BF16)
    rwt_b = router_w.reshape(router_w.shape[1:]).T.astype(BF16)
    sg_b = sh_w_gate.reshape(sh_w_gate.shape[1:]).astype(BF16)
    su_b = sh_w_up.reshape(sh_w_up.shape[1:]).astype(BF16)
    sd_b = sh_w_down.reshape(sh_w_down.shape[1:]).astype(BF16)
    sinks = attn_sinks.reshape(1, N_Q_HEADS)
    cw = conv_w.reshape(CONV_WIDTH, ch)
    cb, clg, clb = row2(conv_b), row2(conv_ln_g), row2(conv_ln_b)

    xp = x_prompt.reshape(tp, d)
    xs_in = x_sample.reshape(ts, d)
    cos_p, sin_p = _rope_tables(jnp.arange(seq))
    cos_s, sin_s = _rope_tables(jnp.tile(PAST_LEN + jnp.arange(dec_seq), dec_batch))

    assert NEW_PAD % dec_seq == 0 and dec_batch % SAMPLE_BB == 0 and seq % WINDOW == 0
    tm_p, tm_s = min(ROW_TILE, seq), min(ROW_TILE, ts)
    tm = min(tm_p, tm_s) // 2
    tn = min(ROUTE_TILE, 2 * tm)
    assert tp % tm == 0 and ts % tm == 0 and t_all % tn == 0
    qp, kp, vp, gp = _in_proj(xp, w_in_b, cos_p, sin_p, tm_p)
    attn_p = _prompt_attn(qp, kp, vp, sinks, batch, seq)
    conv_p = _conv_prompt(gp, cw, cb, clg, clb, batch, seq, WINDOW)
    qs, ks, vs, gs = _in_proj(xs_in, w_in_b, cos_s, sin_s, tm_s)
    ck = cache_win_k.reshape(dec_batch, lc, KV_WIDTH)
    cv = cache_win_v.reshape(dec_batch, lc, KV_WIDTH)
    attn_s = _sample_attn(qs, ks, vs, sinks, ck, cv, dec_batch, dec_seq)
    ext_s = jnp.concatenate([state_conv.reshape(dec_batch, CONV_WIDTH - 1, ch), gs.reshape(dec_batch, dec_seq, ch)], axis=1)
    conv_s = _conv_sample(ext_s, cw, cb, clg, clb, dec_seq).reshape(ts, ch).astype(BF16)

    base, hbp, sct = _mix(attn_p, attn_s, conv_p, conv_s, xp, xs_in, w_out_b, row2(attn_out_g), row2(conv_out_g),
                          row2(ln1_g), row2(ln1_b), rwt_b, sg_b, su_b, sd_b, tm)
    bias_l = jnp.broadcast_to(router_b.reshape(ne, 1).astype(F32), (ne, LANES))
    eidx_t, gate_t, rank_t, cnt = _route(sct, bias_l, tn)

    et = EXPERT_TILE
    n_tiles = t_all * TOP_K // et + ne
    counts = cnt[:, 0]
    tiles_e = (counts + et - 1) // et
    tile_end = jnp.cumsum(tiles_e)
    tile_start = tile_end - tiles_e
    n_active = tile_end[-1]
    last_tile_row = jnp.where(tiles_e > 0, (tile_end - 1) * et, -1)
    fill_plan = jnp.concatenate([last_tile_row, n_active.reshape(1)]).astype(jnp.int32)

    pos3 = _positions((tile_start * et).astype(jnp.int32), eidx_t, rank_t, tm)
    xs = _dispatch(fill_plan, pos3, hbp, n_tiles * et, tm, et)
    wg = exp_w_gate.reshape(exp_w_gate.shape[1:])
    wu = exp_w_up.reshape(exp_w_up.shape[1:])
    wd = exp_w_down.reshape(exp_w_down.shape[1:])
    y_sorted = _moe(xs, wg, wu, wd, tile_start.astype(jnp.int32), tiles_e.astype(jnp.int32), et)
    y_p, y_s = _combine(pos3, y_sorted, base, gate_t.T, row2(ln2_g), row2(ln2_b), tp, tm)

    y_p = y_p.reshape(batch, seq, d)
    y_s = y_s.reshape(dec_batch, dec_seq, d)
    keep = min(WINDOW, seq)
    kv5 = lambda a, b: a.reshape(1, b, -1, N_KV_HEADS, HEAD_DIM)
    new_k_p = kv5(kp.reshape(batch, seq, KV_WIDTH)[:, seq - keep:], batch)
    new_v_p = kv5(vp.reshape(batch, seq, KV_WIDTH)[:, seq - keep:], batch)
    new_c_p = gp.reshape(batch, seq, ch)[:, seq - (CONV_WIDTH - 1):].reshape(1, batch, CONV_WIDTH - 1, ch)
    keep_s = min(WINDOW, PAST_LEN + dec_seq)
    k_all = jnp.concatenate([ck, ks.reshape(dec_batch, dec_seq, KV_WIDTH)], axis=1)
    v_all = jnp.concatenate([cv, vs.reshape(dec_batch, dec_seq, KV_WIDTH)], axis=1)
    new_k_s = kv5(k_all[:, k_all.shape[1] - keep_s:], dec_batch)
    new_v_s = kv5(v_all[:, v_all.shape[1] - keep_s:], dec_batch)
    new_c_s = ext_s[:, dec_seq:].reshape(1, dec_batch, CONV_WIDTH - 1, ch)
    return (y_p, y_s, new_k_p, new_v_p, new_c_p, new_k_s, new_v_s, new_c_s)
```

```python
import functools

import jax
import jax.numpy as jnp
from jax import lax
from jax.experimental import pallas as pl
from jax.experimental.pallas import tpu as pltpu

F32 = jnp.float32
BF16 = jnp.bfloat16

DEPTH = 1
PAST_LEN = 16384
HEAD_DIM = 64
N_Q_HEADS = 16
N_KV_HEADS = 4
Q_PER_KV = N_Q_HEADS // N_KV_HEADS
ATTN_WIDTH = N_Q_HEADS * HEAD_DIM
KV_WIDTH = N_KV_HEADS * HEAD_DIM
WINDOW = 128
ROPE_THETA = 10000.0
CONV_WIDTH = 31
CONV_HALO = 32
TOP_K = 8
N_EXPERT_GROUPS = 8
TOPK_GROUPS = 4
ROUTED_SCALE = 2.5
LN_EPS = 1e-5
RMS_EPS = 1e-6
DEEPNORM_ALPHA = (2.0 * DEPTH) ** 0.25

LANES = 128
VMEM_LIMIT = 56 * 1024 * 1024
ROW_TILE = 512
EXPERT_TILE = 128
PACK_ROWS = 8
ROUTE_TILE = 512
DMA_UNROLL = 8


def _cparams(*sem):
    return pltpu.CompilerParams(dimension_semantics=sem, vmem_limit_bytes=VMEM_LIMIT)


def _dot(a, b):
    return jnp.dot(a, b, preferred_element_type=F32)


def _layer_norm(x, g, b):
    xc = x - jnp.mean(x, axis=-1, keepdims=True)
    var = jnp.mean(xc * xc, axis=-1, keepdims=True)
    return xc * lax.rsqrt(var + LN_EPS) * g + b


def _rms_norm(x, g):
    ms = jnp.mean(x * x, axis=-1, keepdims=True)
    return x * lax.rsqrt(ms + RMS_EPS) * g


def _sigmoid(x):
    return 1.0 / (1.0 + jnp.exp(-x))


def _rope_tables(pos):
    half = HEAD_DIM // 2
    inv_freq = 1.0 / (ROPE_THETA ** (jnp.arange(half, dtype=F32) * (2.0 / HEAD_DIM)))
    ang = pos.astype(F32)[:, None] * inv_freq[None, :]
    cos, sin = jnp.cos(ang), jnp.sin(ang)
    reps = LANES // HEAD_DIM
    return jnp.tile(cos, (1, 2 * reps)), jnp.tile(jnp.concatenate([-sin, sin], axis=1), (1, reps))


def _in_proj_kernel(x_ref, w_ref, cos_ref, sin_ref, q_ref, k_ref, v_ref, g_ref):
    xb = x_ref[...].astype(BF16)
    cos = cos_ref[...]
    sin = sin_ref[...]
    lane = lax.broadcasted_iota(jnp.int32, cos.shape, 1)
    first_half = (lane % HEAD_DIM) < (HEAD_DIM // 2)

    def rope(acc):
        outs = []
        for c in range(acc.shape[1] // LANES):
            a = acc[:, c * LANES:(c + 1) * LANES]
            partner = jnp.where(first_half, pltpu.roll(a, LANES - HEAD_DIM // 2, 1),
                                pltpu.roll(a, HEAD_DIM // 2, 1))
            outs.append(a * cos + partner * sin)
        return jnp.concatenate(outs, axis=1)

    o_k = ATTN_WIDTH
    o_v = o_k + KV_WIDTH
    o_a = o_v + KV_WIDTH
    ch = (w_ref.shape[1] - o_a) // 2
    q = rope(_dot(xb, w_ref[:, 0:o_k]))
    q_ref[...] = (q * (HEAD_DIM ** -0.5)).astype(BF16)
    k_ref[...] = rope(_dot(xb, w_ref[:, o_k:o_v]))
    v_ref[...] = _dot(xb, w_ref[:, o_v:o_a])
    a = _dot(xb, w_ref[:, o_a:o_a + ch])
    gate = _dot(xb, w_ref[:, o_a + ch:o_a + 2 * ch])
    g_ref[...] = a * _sigmoid(gate)


def _in_proj(x2d, w_bf16, cos, sin, tm):
    t, d = x2d.shape
    n = w_bf16.shape[1]
    ch = (n - ATTN_WIDTH - 2 * KV_WIDTH) // 2
    ntab = cos.shape[0] // tm
    return pl.pallas_call(
        _in_proj_kernel,
        grid=(t // tm,),
        in_specs=[
            pl.BlockSpec((tm, d), lambda i: (i, 0)),
            pl.BlockSpec((d, n), lambda i: (0, 0), pipeline_mode=pl.Buffered(1)),
            pl.BlockSpec((tm, LANES), lambda i: (i % ntab, 0)),
            pl.BlockSpec((tm, LANES), lambda i: (i % ntab, 0)),
        ],
        out_specs=[
            pl.BlockSpec((tm, ATTN_WIDTH), lambda i: (i, 0)),
            pl.BlockSpec((tm, KV_WIDTH), lambda i: (i, 0)),
            pl.BlockSpec((tm, KV_WIDTH), lambda i: (i, 0)),
            pl.BlockSpec((tm, ch), lambda i: (i, 0)),
        ],
        out_shape=[
            jax.ShapeDtypeStruct((t, ATTN_WIDTH), BF16),
            jax.ShapeDtypeStruct((t, KV_WIDTH), F32),
            jax.ShapeDtypeStruct((t, KV_WIDTH), F32),
            jax.ShapeDtypeStruct((t, ch), F32),
        ],
        compiler_params=_cparams("parallel"),
        name="in_proj",
    )(x2d, w_bf16, cos, sin)


def _sink_softmax_pv(s, mask, sink, v_bf16):
    s = jnp.where(mask, s, -jnp.inf)
    m = jnp.maximum(jnp.max(s, axis=-1, keepdims=True), sink)
    e = jnp.exp(s - m)
    den = jnp.sum(e, axis=-1, keepdims=True) + jnp.exp(sink - m)
    return _dot(e.astype(BF16), v_bf16) / den


def _prompt_attn_kernel(sink_ref, q_ref, kc_ref, kp_ref, vc_ref, vp_ref, o_ref):
    j = pl.program_id(1)
    q = q_ref[...]
    kb = jnp.concatenate([kp_ref[...], kc_ref[...]], axis=0).astype(BF16)
    vb = jnp.concatenate([vp_ref[...], vc_ref[...]], axis=0).astype(BF16)
    r = lax.broadcasted_iota(jnp.int32, (WINDOW, 2 * WINDOW), 0)
    s_idx = lax.broadcasted_iota(jnp.int32, (WINDOW, 2 * WINDOW), 1)
    mask = (s_idx > r) & (s_idx <= r + WINDOW) & ((s_idx >= WINDOW) | (j > 0))
    outs = []
    for kv in range(N_KV_HEADS):
        kk = kb[:, kv * HEAD_DIM:(kv + 1) * HEAD_DIM]
        vv = vb[:, kv * HEAD_DIM:(kv + 1) * HEAD_DIM]
        for g in range(Q_PER_KV):
            h = kv * Q_PER_KV + g
            qh = q[:, h * HEAD_DIM:(h + 1) * HEAD_DIM]
            s = lax.dot_general(qh, kk, (((1,), (1,)), ((), ())), preferred_element_type=F32)
            outs.append(_sink_softmax_pv(s, mask, sink_ref[0, h], vv))
    o_ref[...] = jnp.concatenate(outs, axis=1).astype(o_ref.dtype)


def _prompt_attn(q, k, v, sinks, batch, seq):
    nb = seq // WINDOW
    cur = lambda b, j: (b * nb + j, 0)
    prev = lambda b, j: (b * nb + jnp.maximum(j - 1, 0), 0)
    return pl.pallas_call(
        _prompt_attn_kernel,
        grid=(batch, nb),
        in_specs=[
            pl.BlockSpec(memory_space=pltpu.SMEM),
            pl.BlockSpec((WINDOW, ATTN_WIDTH), cur),
            pl.BlockSpec((WINDOW, KV_WIDTH), cur),
            pl.BlockSpec((WINDOW, KV_WIDTH), prev),
            pl.BlockSpec((WINDOW, KV_WIDTH), cur),
            pl.BlockSpec((WINDOW, KV_WIDTH), prev),
        ],
        out_specs=pl.BlockSpec((WINDOW, ATTN_WIDTH), cur),
        out_shape=jax.ShapeDtypeStruct((batch * seq, ATTN_WIDTH), BF16),
        compiler_params=_cparams("parallel", "arbitrary"),
        name="prompt_attn",
    )(sinks, q, k, k, v, v)


SAMPLE_BB = 8
NEW_PAD = 8


def _sample_attn_kernel(sink_ref, q_ref, kn_ref, vn_ref, kc_ref, vc_ref, o_ref, *, dec_seq):
    lc = kc_ref.shape[1]
    rows = dec_seq * Q_PER_KV
    per_pad = NEW_PAD // dec_seq
    t_q = lax.broadcasted_iota(jnp.int32, (rows, lc + NEW_PAD), 0) // Q_PER_KV
    col = lax.broadcasted_iota(jnp.int32, (rows, lc + NEW_PAD), 1)
    g_idx = lax.broadcasted_iota(jnp.int32, (rows, 1), 0) % Q_PER_KV
    for bi in range(SAMPLE_BB):
        lo = (bi % per_pad) * dec_seq
        base = (bi // per_pad) * NEW_PAD
        kall = jnp.concatenate([kc_ref[bi], kn_ref[base:base + NEW_PAD, :]], axis=0).astype(BF16)
        vall = jnp.concatenate([vc_ref[bi], vn_ref[base:base + NEW_PAD, :]], axis=0).astype(BF16)
        new_t = col - lc - lo
        mask = ((col < lc) & (col + WINDOW > t_q + lc)) | ((new_t >= 0) & (new_t <= t_q))
        for kv in range(N_KV_HEADS):
            kk = kall[:, kv * HEAD_DIM:(kv + 1) * HEAD_DIM]
            vv = vall[:, kv * HEAD_DIM:(kv + 1) * HEAD_DIM]
            sink = jnp.zeros((rows, 1), F32)
            for g in range(Q_PER_KV):
                sink = jnp.where(g_idx == g, sink_ref[0, kv * Q_PER_KV + g], sink)
            s = lax.dot_general(q_ref[bi, kv], kk, (((1,), (1,)), ((), ())), preferred_element_type=F32)
            o_ref[bi, kv] = _sink_softmax_pv(s, mask, sink, vv)


def _sample_attn(q, k, v, sinks, cache_k, cache_v, dec_batch, dec_seq):
    lc = cache_k.shape[1]
    rows = dec_seq * Q_PER_KV
    q5 = q.reshape(dec_batch, dec_seq, N_KV_HEADS, Q_PER_KV, HEAD_DIM).transpose(0, 2, 1, 3, 4)
    q4 = q5.reshape(dec_batch, N_KV_HEADS, rows, HEAD_DIM)
    blk_q = pl.BlockSpec((SAMPLE_BB, N_KV_HEADS, rows, HEAD_DIM), lambda i: (i, 0, 0, 0))
    blk_new = pl.BlockSpec((SAMPLE_BB * dec_seq, KV_WIDTH), lambda i: (i, 0))
    blk_cache = pl.BlockSpec((SAMPLE_BB, lc, KV_WIDTH), lambda i: (i, 0, 0))
    o = pl.pallas_call(
        functools.partial(_sample_attn_kernel, dec_seq=dec_seq),
        grid=(dec_batch // SAMPLE_BB,),
        in_specs=[pl.BlockSpec(memory_space=pltpu.SMEM), blk_q, blk_new, blk_new, blk_cache, blk_cache],
        out_specs=blk_q,
        out_shape=jax.ShapeDtypeStruct(q4.shape, F32),
        compiler_params=_cparams("parallel"),
        name="sample_attn",
    )(sinks, q4, k, v, cache_k, cache_v)
    o = o.reshape(dec_batch, N_KV_HEADS, dec_seq, Q_PER_KV, HEAD_DIM).transpose(0, 2, 1, 3, 4)
    return o.reshape(dec_batch * dec_seq, ATTN_WIDTH).astype(BF16)


CONV_ROWS = 32
CONV_ROWS_GROUPED = 64


def _conv_taps(ext_ref, w_ref, y_ref, n_out, first_row):
    ch = y_ref.shape[-1]
    for r0 in range(0, n_out, CONV_ROWS):
        nr = min(CONV_ROWS, n_out - r0)
        for c0 in range(0, ch, LANES):
            acc = jnp.zeros((nr, LANES), F32)
            for j in range(CONV_WIDTH):
                acc = acc + ext_ref[pl.ds(first_row + r0 + j, nr), c0:c0 + LANES] * w_ref[j:j + 1, c0:c0 + LANES]
            y_ref[r0:r0 + nr, c0:c0 + LANES] = acc


def _conv_taps_grouped(ext_ref, w_ref, y_ref, n_out, first_row):
    ch = y_ref.shape[-1]
    sub = 8
    groups = [[j for j in range(CONV_WIDTH) if (first_row + j) % sub == res] for res in range(sub)]
    for r0 in range(0, n_out, CONV_ROWS_GROUPED):
        nr = min(CONV_ROWS_GROUPED, n_out - r0)
        for c0 in range(0, ch, LANES):
            y = jnp.zeros((nr, LANES), F32)
            for res, taps in enumerate(groups):
                rows = nr if res == 0 else nr + sub
                acc = jnp.zeros((rows, LANES), F32)
                for j in taps:
                    acc = acc + ext_ref[pl.ds(r0 + first_row + j - res, rows), c0:c0 + LANES] * w_ref[j:j + 1, c0:c0 + LANES]
                if taps:
                    y = y + acc[res:res + nr, :]
            y_ref[r0:r0 + nr, c0:c0 + LANES] = y


def _conv_post(y, b_ref, g_ref, beta_ref):
    z = _layer_norm(y + b_ref[...], g_ref[...], beta_ref[...])
    return z * _sigmoid(z)


def _conv_prompt_kernel(cur_ref, halo_ref, w_ref, b_ref, g_ref, beta_ref, o_ref, ext_ref, y_ref):
    i = pl.program_id(1)
    tt = cur_ref.shape[0]
    ext_ref[0:CONV_HALO, :] = jnp.where(i > 0, halo_ref[...], 0.0)
    ext_ref[CONV_HALO:CONV_HALO + tt, :] = cur_ref[...]
    _conv_taps_grouped(ext_ref, w_ref, y_ref, tt, CONV_HALO - (CONV_WIDTH - 1))
    o_ref[...] = _conv_post(y_ref[...], b_ref, g_ref, beta_ref).astype(o_ref.dtype)


def _conv_prompt(g, w, b, ln_g, ln_b, batch, seq, tt):
    ch = g.shape[1]
    nt = seq // tt
    hb = tt // CONV_HALO
    vec = pl.BlockSpec((1, ch), lambda bi, i: (0, 0))
    return pl.pallas_call(
        _conv_prompt_kernel,
        grid=(batch, nt),
        in_specs=[
            pl.BlockSpec((tt, ch), lambda bi, i: (bi * nt + i, 0)),
            pl.BlockSpec((CONV_HALO, ch), lambda bi, i: (jnp.maximum((bi * nt + i) * hb - 1, 0), 0)),
            pl.BlockSpec((CONV_WIDTH, ch), lambda bi, i: (0, 0)),
            vec, vec, vec,
        ],
        out_specs=pl.BlockSpec((tt, ch), lambda bi, i: (bi * nt + i, 0)),
        out_shape=jax.ShapeDtypeStruct(g.shape, BF16),
        scratch_shapes=[pltpu.VMEM((CONV_HALO + tt, ch), F32), pltpu.VMEM((tt, ch), F32)],
        compiler_params=_cparams("parallel", "arbitrary"),
        name="conv_prompt",
    )(g, g, w, b, ln_g, ln_b)


def _conv_sample_kernel(ext_ref, w_ref, b_ref, g_ref, beta_ref, o_ref, y_ref):
    nb, _, ch = ext_ref.shape
    dec = o_ref.shape[1]
    for bi in range(nb):
        _conv_taps(ext_ref.at[bi], w_ref, y_ref.at[bi], dec, 0)
    for bi in range(nb):
        o_ref[bi] = _conv_post(y_ref[bi], b_ref, g_ref, beta_ref)


def _conv_sample(ext, w, b, ln_g, ln_b, dec_seq):
    nb, rows, ch = ext.shape
    vec = pl.BlockSpec((1, ch), lambda i: (0, 0))
    return pl.pallas_call(
        _conv_sample_kernel,
        grid=(nb // SAMPLE_BB,),
        in_specs=[
            pl.BlockSpec((SAMPLE_BB, rows, ch), lambda i: (i, 0, 0)),
            pl.BlockSpec((CONV_WIDTH, ch), lambda i: (0, 0)),
            vec, vec, vec,
        ],
        out_specs=pl.BlockSpec((SAMPLE_BB, dec_seq, ch), lambda i: (i, 0, 0)),
        out_shape=jax.ShapeDtypeStruct((nb, dec_seq, ch), F32),
        scratch_shapes=[pltpu.VMEM((SAMPLE_BB, dec_seq, ch), F32)],
        compiler_params=_cparams("parallel"),
        name="conv_sample",
    )(ext, w, b, ln_g, ln_b)


def _bf16_words(vals):
    half = vals.shape[1] // 2
    bits = lax.bitcast_convert_type(vals, jnp.uint32)
    return (bits[:, :half] >> 16) | (bits[:, half:] & jnp.uint32(0xFFFF0000))


def _word_halves(words):
    lo = lax.bitcast_convert_type(words << 16, F32)
    hi = lax.bitcast_convert_type(words & jnp.uint32(0xFFFF0000), F32)
    return lo, hi


def _store_packed(ref, words, base=0):
    n = words.shape[0]
    for s in range(PACK_ROWS):
        ref[pl.ds(base + s, n, stride=PACK_ROWS), :] = words[:, s * LANES:(s + 1) * LANES]


def _mix_kernel(ap_ref, as_ref, cp_ref, cs_ref, xp_ref, xs_ref, wo_ref, ag_ref, cg_ref, g1_ref, b1_ref, rwt_ref,
                sg_ref, su_ref, sd_ref, base_ref, hbp_ref, sct_ref, *, n_prompt_tiles):
    is_p = pl.program_id(0) < n_prompt_tiles
    attn = jnp.where(is_p, ap_ref[...], as_ref[...]).astype(F32)
    conv = jnp.where(is_p, cp_ref[...], cs_ref[...]).astype(F32)
    x = jnp.where(is_p, xp_ref[...], xs_ref[...])
    aw = attn.shape[1]
    a = _rms_norm(attn, ag_ref[...]).astype(BF16)
    c = _rms_norm(conv, cg_ref[...]).astype(BF16)
    mixed = _dot(a, wo_ref[0:aw, :]) + _dot(c, wo_ref[aw:, :])
    h = _layer_norm(DEEPNORM_ALPHA * x + mixed, g1_ref[...], b1_ref[...])
    hb = h.astype(BF16)
    _store_packed(hbp_ref, _bf16_words(hb.astype(F32)))
    logits_t = lax.dot_general(rwt_ref[...], hb, (((1,), (1,)), ((), ())), preferred_element_type=F32)
    sct_ref[...] = _sigmoid(logits_t)
    h1 = _dot(hb, sg_ref[...])
    act = (h1 * _sigmoid(h1) * _dot(hb, su_ref[...])).astype(BF16)
    base_ref[...] = DEEPNORM_ALPHA * h + _dot(act, sd_ref[...])


def _mix(attn_p, attn_s, conv_p, conv_s, xp, xs, wo, ag, cg, g1, b1, rwt, sg, su, sd, tm):
    tp, d = xp.shape
    ts = xs.shape[0]
    n_p, n_s = tp // tm, ts // tm
    ne = rwt.shape[0]
    t = tp + ts
    prow = lambda w: pl.BlockSpec((tm, w), lambda i: (jnp.minimum(i, n_p - 1), 0))
    srow = lambda w: pl.BlockSpec((tm, w), lambda i: (jnp.maximum(i - n_p, 0), 0))
    full = lambda a: pl.BlockSpec(a.shape, lambda i: (0,) * a.ndim, pipeline_mode=pl.Buffered(1))
    aw, cw = attn_p.shape[1], conv_p.shape[1]
    return pl.pallas_call(
        functools.partial(_mix_kernel, n_prompt_tiles=n_p),
        grid=(n_p + n_s,),
        in_specs=[prow(aw), srow(aw), prow(cw), srow(cw), prow(d), srow(d)]
        + [full(a) for a in (wo, ag, cg, g1, b1, rwt, sg, su, sd)],
        out_specs=[pl.BlockSpec((tm, d), lambda i: (i, 0)),
                   pl.BlockSpec((tm * PACK_ROWS, LANES), lambda i: (i, 0)),
                   pl.BlockSpec((ne, tm), lambda i: (0, i))],
        out_shape=[jax.ShapeDtypeStruct((t, d), F32),
                   jax.ShapeDtypeStruct((t * PACK_ROWS, LANES), jnp.uint32),
                   jax.ShapeDtypeStruct((ne, t), F32)],
        compiler_params=_cparams("arbitrary"),
        name="mix_ln1",
    )(attn_p, attn_s, conv_p, conv_s, xp, xs, wo, ag, cg, g1, b1, rwt, sg, su, sd)


def _route_kernel(sc_ref, bias_ref, tri_ref, eidx_ref, gate_ref, rank_ref, cnt_ref, carry_ref, mask_ref, eidf_ref):
    ne, tn = sc_ref.shape
    per = ne // N_EXPERT_GROUPS
    ninf = -jnp.inf

    @pl.when(pl.program_id(0) == 0)
    def _():
        carry_ref[...] = jnp.zeros_like(carry_ref)

    erow = lax.broadcasted_iota(jnp.int32, (ne, LANES), 0).astype(F32)
    grow = lax.broadcasted_iota(jnp.int32, (per, LANES), 0).astype(F32)
    gi = lax.broadcasted_iota(jnp.int32, (N_EXPERT_GROUPS, LANES), 0).astype(F32)
    colmax = lambda v: jnp.max(v, axis=0, keepdims=True)
    first_at = lambda v, m, idx, n: jnp.min(jnp.where(v == m, idx, float(n)), axis=0, keepdims=True)

    for c in range(tn // LANES):
        lanes = slice(c * LANES, (c + 1) * LANES)
        sc = sc_ref[:, lanes]
        sel = sc + bias_ref[...]
        gs = []
        for g in range(N_EXPERT_GROUPS):
            v = sel[g * per:(g + 1) * per, :]
            m1 = colmax(v)
            m2 = colmax(jnp.where(grow == first_at(v, m1, grow, per), ninf, v))
            gs.append(m1 + m2)
        grp = jnp.concatenate(gs, axis=0)
        gsel = jnp.zeros_like(grp)
        for _ in range(TOPK_GROUPS):
            m = colmax(grp)
            hit = gi == first_at(grp, m, gi, N_EXPERT_GROUPS)
            gsel = jnp.where(hit, 1.0, gsel)
            grp = jnp.where(hit, ninf, grp)
        emask = jnp.concatenate([jnp.broadcast_to(gsel[g:g + 1, :], (per, LANES)) for g in range(N_EXPERT_GROUPS)], axis=0)
        masked = jnp.where(emask > 0.5, sel, ninf)
        selm = jnp.zeros_like(sel)
        idxs, gts = [], []
        for _ in range(TOP_K):
            ix = first_at(masked, colmax(masked), erow, ne)
            hit = erow == ix
            gts.append(jnp.sum(jnp.where(hit, sc, 0.0), axis=0, keepdims=True))
            idxs.append(ix)
            masked = jnp.where(hit, ninf, masked)
            selm = jnp.where(hit, 1.0, selm)
        gsum = gts[0]
        for g in gts[1:]:
            gsum = gsum + g
        eidf = jnp.concatenate(idxs, axis=0)
        eidf_ref[:, lanes] = eidf
        eidx_ref[:, lanes] = eidf.astype(jnp.int32)
        gate_ref[:, lanes] = jnp.concatenate([g / gsum * ROUTED_SCALE for g in gts], axis=0)
        mask_ref[:, lanes] = selm

    selm_all = mask_ref[...]
    incl = _dot(selm_all.astype(BF16), tri_ref[...])
    carry = carry_ref[...]
    for c in range(tn // LANES):
        lanes = slice(c * LANES, (c + 1) * LANES)
        before = incl[:, lanes] - selm_all[:, lanes] + carry
        eidf = eidf_ref[:, lanes]
        rk = [jnp.sum(jnp.where(erow == eidf[k:k + 1, :], before, 0.0), axis=0, keepdims=True) for k in range(TOP_K)]
        rank_ref[:, lanes] = jnp.concatenate(rk, axis=0).astype(jnp.int32)
    total = carry + jnp.broadcast_to(incl[:, tn - 1:tn], (ne, LANES))
    carry_ref[...] = total
    cnt_ref[...] = total.astype(jnp.int32)


def _route(sct, bias_l, tn):
    ne, t = sct.shape
    tri = jnp.triu(jnp.ones((tn, tn), BF16))
    col = pl.BlockSpec((TOP_K, tn), lambda i: (0, i))
    return pl.pallas_call(
        _route_kernel,
        grid=(t // tn,),
        in_specs=[pl.BlockSpec((ne, tn), lambda i: (0, i)),
                  pl.BlockSpec((ne, LANES), lambda i: (0, 0)),
                  pl.BlockSpec((tn, tn), lambda i: (0, 0))],
        out_specs=[col, col, col, pl.BlockSpec((ne, LANES), lambda i: (0, 0))],
        out_shape=[jax.ShapeDtypeStruct((TOP_K, t), jnp.int32), jax.ShapeDtypeStruct((TOP_K, t), F32),
                   jax.ShapeDtypeStruct((TOP_K, t), jnp.int32), jax.ShapeDtypeStruct((ne, LANES), jnp.int32)],
        scratch_shapes=[pltpu.VMEM((ne, LANES), F32), pltpu.VMEM((ne, tn), F32), pltpu.VMEM((TOP_K, tn), F32)],
        compiler_params=_cparams("arbitrary"),
        name="route",
    )(sct, bias_l, tri)


def _pos_kernel(pstart_ref, eidx_ref, rank_ref, pos_ref):
    eid = eidx_ref[...]
    base = lax.fori_loop(0, pstart_ref.shape[0], lambda e, acc: jnp.where(eid == e, pstart_ref[e], acc),
                         jnp.zeros_like(eid), unroll=DMA_UNROLL)
    pos_ref[...] = base + rank_ref[...]


def _positions(pstart, eidx_t, rank_t, tm):
    t = eidx_t.shape[1]
    col = pl.BlockSpec((TOP_K, tm), lambda i, ps: (0, i))
    return pl.pallas_call(
        _pos_kernel,
        grid_spec=pltpu.PrefetchScalarGridSpec(
            num_scalar_prefetch=1, grid=(t // tm,), in_specs=[col, col],
            out_specs=pl.BlockSpec((None, TOP_K, tm), lambda i, ps: (i, 0, 0))),
        out_shape=jax.ShapeDtypeStruct((t // tm, TOP_K, tm), jnp.int32),
        compiler_params=_cparams("parallel"),
        name="positions",
    )(pstart, eidx_t, rank_t)


def _packed_rows(ref, row):
    return ref.at[pl.ds(pl.multiple_of(row * PACK_ROWS, PACK_ROWS), PACK_ROWS), :]


def _dispatch_kernel(last_ref, pos_hbm, hb_ref, xs_hbm, pos_s, zero_ref, sem_pos, sem_zero, sem_row):
    i = pl.program_id(0)
    tm = pos_s.shape[1]
    tile_rows = zero_ref.shape[0]

    @pl.when(i == 0)
    def _():
        zero_ref[...] = jnp.zeros_like(zero_ref)

        def fill(e):
            start = pl.multiple_of(last_ref[e] * PACK_ROWS, PACK_ROWS)
            return pltpu.make_async_copy(zero_ref, xs_hbm.at[pl.ds(start, tile_rows), :], sem_zero)

        def go(e, c):
            @pl.when(last_ref[e] >= 0)
            def _():
                fill(e).start()
            return c

        def done(e, c):
            @pl.when(last_ref[e] >= 0)
            def _():
                fill(e).wait()
            return c

        def tail(t):
            start = pl.multiple_of(t * tile_rows, PACK_ROWS)
            return pltpu.make_async_copy(zero_ref, xs_hbm.at[pl.ds(start, tile_rows), :], sem_zero)

        def tail_go(t, c):
            tail(t).start()
            return c

        def tail_done(t, c):
            tail(t).wait()
            return c

        n_experts = last_ref.shape[0] - 1
        n_active, n_tiles = last_ref[n_experts], xs_hbm.shape[0] // tile_rows
        lax.fori_loop(0, n_experts, go, 0)
        lax.fori_loop(n_active, n_tiles, tail_go, 0)
        lax.fori_loop(0, n_experts, done, 0)
        lax.fori_loop(n_active, n_tiles, tail_done, 0)

    get_pos = pltpu.make_async_copy(pos_hbm.at[i], pos_s, sem_pos)
    get_pos.start()
    get_pos.wait()

    def row_copy(r, k):
        return pltpu.make_async_copy(_packed_rows(hb_ref, r), _packed_rows(xs_hbm, pos_s[k, r]), sem_row)

    def issue(r, c):
        for k in range(TOP_K):
            row_copy(r, k).start(priority=k % 2)
        return c

    def drain(r, c):
        for k in range(TOP_K):
            row_copy(r, k).wait()
        return c

    lax.fori_loop(0, tm, issue, 0, unroll=DMA_UNROLL)
    lax.fori_loop(0, tm, drain, 0, unroll=DMA_UNROLL)


def _dispatch(fill_plan, pos3, hbp, n_rows, tm, expert_tile):
    n_tok_tiles = pos3.shape[0]
    return pl.pallas_call(
        _dispatch_kernel,
        grid_spec=pltpu.PrefetchScalarGridSpec(
            num_scalar_prefetch=1, grid=(n_tok_tiles,),
            in_specs=[pl.BlockSpec(memory_space=pl.ANY),
                      pl.BlockSpec((tm * PACK_ROWS, LANES), lambda i, lt: (i, 0))],
            out_specs=pl.BlockSpec(memory_space=pl.ANY),
            scratch_shapes=[pltpu.SMEM((TOP_K, tm), jnp.int32),
                            pltpu.VMEM((expert_tile * PACK_ROWS, LANES), jnp.uint32),
                            pltpu.SemaphoreType.DMA(()), pltpu.SemaphoreType.DMA(()), pltpu.SemaphoreType.DMA(())]),
        out_shape=jax.ShapeDtypeStruct((n_rows * PACK_ROWS, LANES), jnp.uint32),
        compiler_params=_cparams("arbitrary"),
        name="dispatch",
    )(fill_plan, pos3, hbp)


def _moe_kernel(first_ref, count_ref, x_hbm, wg_ref, wu_ref, wd_ref, y_hbm, xbuf, ybuf, xrem, yrem, wgb, wub, wdb,
                pend, sem_x, sem_y, sem_r):
    e = pl.program_id(0)
    n, t0 = count_ref[e], first_ref[e]
    tile_rows = xrem.shape[0]
    pair_rows = 2 * tile_rows
    n_pairs = n // 2
    odd = n - 2 * n_pairs

    def hbm_rows(ref, tile, rows):
        return ref.at[pl.ds(pl.multiple_of(tile * tile_rows, PACK_ROWS), rows), :]

    def slot_rows(ref, slot):
        return ref.at[pl.ds(pl.multiple_of(slot * pair_rows, PACK_ROWS), pair_rows), :]

    def x_copy(j, slot):
        return pltpu.make_async_copy(hbm_rows(x_hbm, t0 + 2 * j, pair_rows), slot_rows(xbuf, slot), sem_x.at[slot])

    def y_copy(first, j, slot):
        return pltpu.make_async_copy(slot_rows(ybuf, slot), hbm_rows(y_hbm, first + 2 * j, pair_rows), sem_y.at[slot])

    def y_last(first, pairs):
        return pltpu.make_async_copy(yrem, hbm_rows(y_hbm, first + 2 * pairs, tile_rows), sem_r.at[1])

    x_last = pltpu.make_async_copy(hbm_rows(x_hbm, t0 + 2 * n_pairs, tile_rows), xrem, sem_r.at[0])

    def drain_previous():
        first, count = pend[1], pend[2]
        pairs = count // 2

        @pl.when(pairs >= 2)
        def _():
            y_copy(first, pairs - 2, pairs % 2).wait()

        @pl.when(pairs >= 1)
        def _():
            y_copy(first, pairs - 1, (pairs - 1) % 2).wait()

        @pl.when(count - 2 * pairs == 1)
        def _():
            y_last(first, pairs).wait()

        pend[0] = 0

    def experts_mlp(src, src_base, dst, dst_base, rows):
        tm = rows // PACK_ROWS
        halves = [_word_halves(src[pl.ds(src_base + s, tm, stride=PACK_ROWS), :]) for s in range(PACK_ROWS)]
        x = jnp.concatenate([h[0] for h in halves] + [h[1] for h in halves], axis=1).astype(BF16)
        h1 = _dot(x, wg_ref[...])
        act = (h1 * _sigmoid(h1) * _dot(x, wu_ref[...])).astype(BF16)
        y = _dot(act, wd_ref[...]).astype(BF16)
        _store_packed(dst, _bf16_words(y.astype(F32)), dst_base)

    @pl.when(e == 0)
    def _():
        pend[0] = 0

    @pl.when(n > 0)
    def _():
        @pl.when(n_pairs > 0)
        def _():
            x_copy(0, 0).start()

        @pl.when(odd == 1)
        def _():
            x_last.start()

        @pl.when(pend[0] == 1)
        def _():
            drain_previous()

        @pl.when(odd == 1)
        def _():
            x_last.wait()
            experts_mlp(xrem, 0, yrem, 0, tile_rows)
            y_last(t0, n_pairs).start()

        def step(j, c):
            slot = j % 2
            x_copy(j, slot).wait()

            @pl.when(j + 1 < n_pairs)
            def _():
                x_copy(j + 1, 1 - slot).start()

            @pl.when(j >= 2)
            def _():
                y_copy(t0, j - 2, slot).wait()

            experts_mlp(xbuf, slot * pair_rows, ybuf, slot * pair_rows, pair_rows)
            y_copy(t0, j, slot).start()
            return c

        lax.fori_loop(0, n_pairs, step, 0)
        pend[0] = 1
        pend[1] = t0
        pend[2] = n

    @pl.when((e == pl.num_programs(0) - 1) & (pend[0] == 1))
    def _():
        drain_previous()


def _moe(xs, w_gate, w_up, w_down, tile_start, tiles_e, tm):
    ne, d, ff = w_gate.shape
    tile_rows = tm * PACK_ROWS
    wmap = lambda e, ts, nt: (e, 0, 0)
    return pl.pallas_call(
        _moe_kernel,
        grid_spec=pltpu.PrefetchScalarGridSpec(
            num_scalar_prefetch=2,
            grid=(ne,),
            in_specs=[
                pl.BlockSpec(memory_space=pl.ANY),
                pl.BlockSpec((None, d, ff), wmap),
                pl.BlockSpec((None, d, ff), wmap),
                pl.BlockSpec((None, ff, d), wmap),
            ],
            out_specs=pl.BlockSpec(memory_space=pl.ANY),
            scratch_shapes=[pltpu.VMEM((4 * tile_rows, LANES), jnp.uint32), pltpu.VMEM((4 * tile_rows, LANES), jnp.uint32),
                            pltpu.VMEM((tile_rows, LANES), jnp.uint32), pltpu.VMEM((tile_rows, LANES), jnp.uint32),
                            pltpu.VMEM((d, ff), BF16), pltpu.VMEM((d, ff), BF16), pltpu.VMEM((ff, d), BF16),
                            pltpu.SMEM((3,), jnp.int32),
                            pltpu.SemaphoreType.DMA((2,)), pltpu.SemaphoreType.DMA((2,)), pltpu.SemaphoreType.DMA((2,))],
        ),
        out_shape=jax.ShapeDtypeStruct(xs.shape, jnp.uint32),
        input_output_aliases={2: 0},
        compiler_params=_cparams("arbitrary"),
        name="moe_experts",
    )(tile_start, tiles_e, xs, w_gate, w_up, w_down)


COMBINE_ROWS = 64


def _combine_kernel(pos_hbm, y_hbm, base_ref, gate_ref, g_ref, b_ref, op_ref, os_ref,
                    pos_s, ybuf, gb_ref, f_ref, sem_pos, sem_row, *, n_prompt_tiles):
    i = pl.program_id(0)
    tm, d = base_ref.shape
    slot = i % 2
    slot_tokens = TOP_K * tm

    def row_copy(s, r, k):
        return pltpu.make_async_copy(_packed_rows(y_hbm, pos_s[s, k, r]),
                                     _packed_rows(ybuf, s * slot_tokens + k * tm + r), sem_row.at[s])

    def start_gather(tile, s):
        get_pos = pltpu.make_async_copy(pos_hbm.at[tile], pos_s.at[s], sem_pos)
        get_pos.start()
        get_pos.wait()

        def issue(r, c):
            for k in range(TOP_K):
                row_copy(s, r, k).start(priority=k % 2)
            return c

        lax.fori_loop(0, tm, issue, 0, unroll=DMA_UNROLL)

    @pl.when(i == 0)
    def _():
        start_gather(0, 0)

    @pl.when(i + 1 < pl.num_programs(0))
    def _():
        start_gather(i + 1, 1 - slot)

    gates = gate_ref[...]
    for k in range(TOP_K):
        gb_ref[k] = jnp.broadcast_to(gates[:, k:k + 1], (tm, LANES))

    def drain(r, c):
        for k in range(TOP_K):
            row_copy(slot, r, k).wait()
        return c

    lax.fori_loop(0, tm, drain, 0, unroll=DMA_UNROLL)

    for r0 in range(0, tm, COMBINE_ROWS):
        nr = min(COMBINE_ROWS, tm - r0)
        for s in range(PACK_ROWS):
            c_lo, c_hi = s * LANES, (PACK_ROWS + s) * LANES
            lo = base_ref[r0:r0 + nr, c_lo:c_lo + LANES]
            hi = base_ref[r0:r0 + nr, c_hi:c_hi + LANES]
            for k in range(TOP_K):
                first = (slot * slot_tokens + k * tm + r0) * PACK_ROWS + s
                w_lo, w_hi = _word_halves(ybuf[pl.ds(first, nr, stride=PACK_ROWS), :])
                g = gb_ref[k, r0:r0 + nr, :]
                lo = lo + g * w_lo
                hi = hi + g * w_hi
            f_ref[r0:r0 + nr, c_lo:c_lo + LANES] = lo
            f_ref[r0:r0 + nr, c_hi:c_hi + LANES] = hi
    y = _layer_norm(f_ref[...], g_ref[...], b_ref[...])

    @pl.when(i < n_prompt_tiles)
    def _():
        op_ref[...] = y

    @pl.when(i >= n_prompt_tiles)
    def _():
        os_ref[...] = y


def _combine(pos3, y_sorted, base, gates, ln_g, ln_b, tp, tm):
    t, d = base.shape
    n_p = tp // tm
    vec = pl.BlockSpec((1, d), lambda i: (0, 0))
    return pl.pallas_call(
        functools.partial(_combine_kernel, n_prompt_tiles=n_p),
        grid=(t // tm,),
        in_specs=[pl.BlockSpec(memory_space=pl.ANY), pl.BlockSpec(memory_space=pl.ANY),
                  pl.BlockSpec((tm, d), lambda i: (i, 0)), pl.BlockSpec((tm, TOP_K), lambda i: (i, 0)), vec, vec],
        out_specs=[pl.BlockSpec((tm, d), lambda i: (jnp.minimum(i, n_p - 1), 0)),
                   pl.BlockSpec((tm, d), lambda i: (jnp.maximum(i - n_p, 0), 0))],
        out_shape=[jax.ShapeDtypeStruct((tp, d), F32), jax.ShapeDtypeStruct((t - tp, d), F32)],
        scratch_shapes=[pltpu.SMEM((2, TOP_K, tm), jnp.int32),
                        pltpu.VMEM((2 * TOP_K * tm * PACK_ROWS, LANES), jnp.uint32),
                        pltpu.VMEM((TOP_K, tm, LANES), F32),
                        pltpu.VMEM((tm, d), F32),
                        pltpu.SemaphoreType.DMA(()), pltpu.SemaphoreType.DMA((2,))],
        compiler_params=_cparams("arbitrary"),
        name="combine_ln2",
    )(pos3, y_sorted, base, gates, ln_g, ln_b)


def kernel(x_prompt, x_sample, cache_win_k, cache_win_v, state_conv, w_in, attn_sinks, conv_w, conv_b, conv_ln_g, conv_ln_b, attn_out_g, conv_out_g, w_out, ln1_g, ln1_b, router_w, router_b, exp_w_gate, exp_w_up, exp_w_down, sh_w_gate, sh_w_up, sh_w_down, ln2_g, ln2_b):
    assert w_in.shape[0] == DEPTH
    batch, seq, d = x_prompt.shape
    dec_batch, dec_seq, _ = x_sample.shape
    lc = cache_win_k.shape[2]
    ch = conv_w.shape[2]
    ne = router_w.shape[2]
    tp, ts = batch * seq, dec_batch * dec_seq
    t_all = tp + ts
    assert d == 2 * PACK_ROWS * LANES
    row2 = lambda a: a.reshape(1, a.shape[-1])

    w_in_b = w_in.reshape(w_in.shape[1:]).astype(BF16)
    w_out_b = w_out.reshape(w_out.shape[1:]).astype(BF16)
    rwt_b = router_w.reshape(router_w.shape[1:]).T.astype(BF16)
    sg_b = sh_w_gate.reshape(sh_w_gate.shape[1:]).astype(BF16)
    su_b = sh_w_up.reshape(sh_w_up.shape[1:]).astype(BF16)
    sd_b = sh_w_down.reshape(sh_w_down.shape[1:]).astype(BF16)
    sinks = attn_sinks.reshape(1, N_Q_HEADS)
    cw = conv_w.reshape(CONV_WIDTH, ch)
    cb, clg, clb = row2(conv_b), row2(conv_ln_g), row2(conv_ln_b)

    xp = x_prompt.reshape(tp, d)
    xs_in = x_sample.reshape(ts, d)
    cos_p, sin_p = _rope_tables(jnp.arange(seq))
    cos_s, sin_s = _rope_tables(jnp.tile(PAST_LEN + jnp.arange(dec_seq), dec_batch))

    assert NEW_PAD % dec_seq == 0 and dec_batch % SAMPLE_BB == 0 and seq % WINDOW == 0
    tm_p, tm_s = min(ROW_TILE, seq), min(ROW_TILE, ts)
    tm = min(tm_p, tm_s) // 2
    tn = min(ROUTE_TILE, 2 * tm)
    assert tp % tm == 0 and ts % tm == 0 and t_all % tn == 0
    qp, kp, vp, gp = _in_proj(xp, w_in_b, cos_p, sin_p, tm_p)
    attn_p = _prompt_attn(qp, kp, vp, sinks, batch, seq)
    conv_p = _conv_prompt(gp, cw, cb, clg, clb, batch, seq, WINDOW)
    qs, ks, vs, gs = _in_proj(xs_in, w_in_b, cos_s, sin_s, tm_s)
    ck = cache_win_k.reshape(dec_batch, lc, KV_WIDTH)
    cv = cache_win_v.reshape(dec_batch, lc, KV_WIDTH)
    attn_s = _sample_attn(qs, ks, vs, sinks, ck, cv, dec_batch, dec_seq)
    ext_s = jnp.concatenate([state_conv.reshape(dec_batch, CONV_WIDTH - 1, ch), gs.reshape(dec_batch, dec_seq, ch)], axis=1)
    conv_s = _conv_sample(ext_s, cw, cb, clg, clb, dec_seq).reshape(ts, ch).astype(BF16)

    base, hbp, sct = _mix(attn_p, attn_s, conv_p, conv_s, xp, xs_in, w_out_b, row2(attn_out_g), row2(conv_out_g),
                          row2(ln1_g), row2(ln1_b), rwt_b, sg_b, su_b, sd_b, tm)
    bias_l = jnp.broadcast_to(router_b.reshape(ne, 1).astype(F32), (ne, LANES))
    eidx_t, gate_t, rank_t, cnt = _route(sct, bias_l, tn)

    et = EXPERT_TILE
    n_tiles = t_all * TOP_K // et + ne
    counts = cnt[:, 0]
    tiles_e = (counts + et - 1) // et
    tile_end = jnp.cumsum(tiles_e)
    tile_start = tile_end - tiles_e
    n_active = tile_end[-1]
    last_tile_row = jnp.where(tiles_e > 0, (tile_end - 1) * et, -1)
    fill_plan = jnp.concatenate([last_tile_row, n_active.reshape(1)]).astype(jnp.int32)

    pos3 = _positions((tile_start * et).astype(jnp.int32), eidx_t, rank_t, tm)
    xs = _dispatch(fill_plan, pos3, hbp, n_tiles * et, tm, et)
    wg = exp_w_gate.reshape(exp_w_gate.shape[1:])
    wu = exp_w_up.reshape(exp_w_up.shape[1:])
    wd = exp_w_down.reshape(exp_w_down.shape[1:])
    y_sorted = _moe(xs, wg, wu, wd, tile_start.astype(jnp.int32), tiles_e.astype(jnp.int32), et)
    y_p, y_s = _combine(pos3, y_sorted, base, gate_t.T, row2(ln2_g), row2(ln2_b), tp, tm)

    y_p = y_p.reshape(batch, seq, d)
    y_s = y_s.reshape(dec_batch, dec_seq, d)
    keep = min(WINDOW, seq)
    kv5 = lambda a, b: a.reshape(1, b, -1, N_KV_HEADS, HEAD_DIM)
    new_k_p = kv5(kp.reshape(batch, seq, KV_WIDTH)[:, seq - keep:], batch)
    new_v_p = kv5(vp.reshape(batch, seq, KV_WIDTH)[:, seq - keep:], batch)
    new_c_p = gp.reshape(batch, seq, ch)[:, seq - (CONV_WIDTH - 1):].reshape(1, batch, CONV_WIDTH - 1, ch)
    keep_s = min(WINDOW, PAST_LEN + dec_seq)
    k_all = jnp.concatenate([ck, ks.reshape(dec_batch, dec_seq, KV_WIDTH)], axis=1)
    v_all = jnp.concatenate([cv, vs.reshape(dec_batch, dec_seq, KV_WIDTH)], axis=1)
    new_k_s = kv5(k_all[:, k_all.shape[1] - keep_s:], dec_batch)
    new_v_s = kv5(v_all[:, v_all.shape[1] - keep_s:], dec_batch)
    new_c_s = ext_s[:, dec_seq:].reshape(1, dec_batch, CONV_WIDTH - 1, ch)
    return (y_p, y_s, new_k_p, new_v_p, new_c_p, new_k_s, new_v_s, new_c_s)
```
